```python
import math
import jax, jax.numpy as jnp
from jax import lax
import numpy as np

D_MODEL = 1024
BATCH = 4
SEQ = 8192
DEPTH = 1

CHUNK = 64
Q_BLOCK = 128
FOX_HEADS = 8
FOX_HEAD_DIM = 64
FOX_WIDTH = FOX_HEADS * FOX_HEAD_DIM
RWKV_HEADS = 8
RWKV_HEAD_DIM = 64
RWKV_WIDTH = RWKV_HEADS * RWKV_HEAD_DIM
W_LORA = 64
A_LORA = 64
G_LORA = 128
GN_EPS = 64e-5
PEER_HEADS = 8
PEER_N_KEYS = 128
PEER_N_EXPERTS = PEER_N_KEYS * PEER_N_KEYS
PEER_HALF = 128
PEER_TOPK = 16
PEER_TOKEN_BLOCK = 128
DN_ALPHA = (2 * DEPTH) ** 0.25
DN_BETA = (8 * DEPTH) ** -0.25
LN_EPS = 1e-5
FOX_SPLITS = (FOX_WIDTH, FOX_WIDTH, FOX_WIDTH, FOX_HEADS)
RWKV_SPLITS = (RWKV_WIDTH, RWKV_WIDTH, RWKV_WIDTH, W_LORA, A_LORA, G_LORA)
FOX_COLS = 3 * FOX_WIDTH + FOX_HEADS
RWKV_COLS = 3 * RWKV_WIDTH + W_LORA + A_LORA + G_LORA
IN_WIDTH = FOX_COLS + RWKV_COLS + 2 * D_MODEL

kernel_name = "fox_rwkv7_peer_deepnorm_hybrid"


def _split(t, sizes):
    out = []
    start = 0
    for n in sizes:
        out.append(t[..., start:start + n])
        start += n
    return out


def _layer_norm(x, g, b):
    xf = x.astype(jnp.float32)
    mu = jnp.mean(xf, axis=-1, keepdims=True)
    var = jnp.mean(jnp.square(xf - mu), axis=-1, keepdims=True)
    return ((xf - mu) * lax.rsqrt(var + LN_EPS) * g + b).astype(x.dtype)


def _token_shift(p, mu):
    prev = jnp.pad(p, ((0, 0), (1, 0), (0, 0)))[:, :-1]
    return p + mu * (prev - p)


def _forgetting_attention(q, k, v, f_logit):
    B, S = q.shape[0], q.shape[1]
    heads = lambda t: t.reshape(B, S, FOX_HEADS, FOX_HEAD_DIM).transpose(0, 2, 1, 3)
    q, k, v = heads(q), heads(k), heads(v)
    log_f = jax.nn.log_sigmoid(f_logit.astype(jnp.float32))
    c = jnp.cumsum(log_f, axis=1).transpose(0, 2, 1)
    scale = FOX_HEAD_DIM ** -0.5
    outs = []
    for blk in range(S // Q_BLOCK):
        q0 = blk * Q_BLOCK
        q1 = q0 + Q_BLOCK
        s = jnp.einsum('bhqd,bhkd->bhqk', q[:, :, q0:q1], k[:, :, :q1]).astype(jnp.float32) * scale
        s = s + c[:, :, q0:q1, None] - c[:, :, None, :q1]
        causal = jnp.arange(q0, q1)[:, None] >= jnp.arange(q1)[None, :]
        s = jnp.where(causal, s, -jnp.inf)
        p = jax.nn.softmax(s, axis=-1).astype(v.dtype)
        outs.append(jnp.einsum('bhqk,bhkd->bhqd', p, v[:, :, :q1]))
    o = jnp.concatenate(outs, axis=2)
    return o.transpose(0, 2, 1, 3).reshape(B, S, FOX_WIDTH)


def _rwkv7_time_mix(r, k, v, w_lora, a_lora, g_lora, w0, w2, a0, a2, g2, k_k, k_a, r_k, ln_g, ln_b):
    B, S = r.shape[0], r.shape[1]
    f32 = jnp.float32
    H, N = RWKV_HEADS, RWKV_HEAD_DIM
    w_pre = (w0 + jnp.tanh(w_lora) @ w2).astype(f32)
    decay = jnp.exp(-jnp.exp(-jax.nn.softplus(-w_pre) - 0.5))
    a = jax.nn.sigmoid((a0 + a_lora @ a2).astype(f32))
    g = jax.nn.sigmoid(g_lora) @ g2
    heads = lambda t: t.astype(f32).reshape(B, S, H, N)
    kk = heads(k * k_k)
    kk = kk / jnp.maximum(jnp.sqrt(jnp.sum(kk * kk, axis=-1, keepdims=True)), 1e-12)
    k_mod = k.astype(f32) * (1.0 + (a - 1.0) * k_a.astype(f32))
    r_h, k_h, v_h, w_h, a_h = heads(r), heads(k_mod), heads(v), heads(decay), heads(a)

    def to_chunks(t):
        return t.transpose(1, 0, 2, 3).reshape(S // CHUNK, CHUNK, B, H, N)

    def frame_step(state, inp):
        r_t, w_t, k_t, v_t, kk_t, a_t = inp
        sa = jnp.einsum('bhij,bhj->bhi', state, -kk_t)
        state = (state * w_t[:, :, None, :] + sa[..., None] * (kk_t * a_t)[:, :, None, :]
                 + v_t[..., None] * k_t[:, :, None, :])
        return state, jnp.einsum('bhij,bhj->bhi', state, r_t)

    def chunk_step(state, chunk_inp):
        return lax.scan(frame_step, state, chunk_inp)

    state0 = jnp.zeros((B, H, N, N), f32)
    xs = (to_chunks(r_h), to_chunks(w_h), to_chunks(k_h), to_chunks(v_h), to_chunks(kk), to_chunks(a_h))
    _, y = lax.scan(chunk_step, state0, xs)
    y = y.reshape(S, B, H, N).transpose(1, 0, 2, 3)
    mu = jnp.mean(y, axis=-1, keepdims=True)
    var = jnp.mean(jnp.square(y - mu), axis=-1, keepdims=True)
    y = ((y - mu) * lax.rsqrt(var + GN_EPS)).reshape(B, S, RWKV_WIDTH) * ln_g + ln_b
    bonus = jnp.sum(r_h * k_h * r_k.astype(f32).reshape(H, N), axis=-1, keepdims=True) * v_h
    y = (y + bonus.reshape(B, S, RWKV_WIDTH)) * g
    return y.astype(r.dtype)


def _peer(x, w_q, sub_keys, u_table, v_table):
    B, S, D = x.shape
    xt = x.reshape((B * S) // PEER_TOKEN_BLOCK, PEER_TOKEN_BLOCK, D)
    K = PEER_TOPK

    def block(xb):
        t = xb.shape[0]
        q = (xb @ w_q).reshape(t, PEER_HEADS, 2, PEER_HALF)
        s = jnp.einsum('thpd,hpkd->thpk', q, sub_keys).astype(jnp.float32)
        top_s, top_i = lax.top_k(s, K)
        cand_s = top_s[:, :, 0, :, None] + top_s[:, :, 1, None, :]
        cand_i = top_i[:, :, 0, :, None] * PEER_N_KEYS + top_i[:, :, 1, None, :]
        best_s, best_pos = lax.top_k(cand_s.reshape(t, PEER_HEADS, K * K), K)
        idx = jnp.take_along_axis(cand_i.reshape(t, PEER_HEADS, K * K), best_pos, axis=-1)
        gate = jax.nn.softmax(best_s, axis=-1).astype(xb.dtype)
        h = jax.nn.gelu(jnp.einsum('thkd,td->thk', u_table[idx], xb), approximate=False)
        return jnp.einsum('thk,thkd->td', gate * h, v_table[idx])

    return lax.map(block, xt).reshape(B, S, D)


def setup_inputs(seed: int = 0) -> dict:
    key = jax.random.key(seed)
    ks = jax.random.split(key, 32)
    L = DEPTH
    nrm = lambda k, shape, scale: jax.random.normal(k, shape, jnp.float32) * scale
    return {
        "x": nrm(ks[0], (BATCH, SEQ, D_MODEL), 1.0),
        "w_in": nrm(ks[1], (L, D_MODEL, IN_WIDTH), D_MODEL ** -0.5),
        "fox_f_bias": 3.0 + nrm(ks[2], (L, FOX_HEADS), 0.5),
        "rwkv_mu": jax.random.uniform(ks[3], (L, RWKV_COLS), jnp.float32),
        "rwkv_w0": nrm(ks[4], (L, RWKV_WIDTH), 1.0),
        "rwkv_w2": nrm(ks[5], (L, W_LORA, RWKV_WIDTH), 0.1 * W_LORA ** -0.5),
        "rwkv_a0": nrm(ks[6], (L, RWKV_WIDTH), 0.5),
        "rwkv_a2": nrm(ks[7], (L, A_LORA, RWKV_WIDTH), A_LORA ** -0.5),
        "rwkv_g2": nrm(ks[8], (L, G_LORA, RWKV_WIDTH), G_LORA ** -0.5),
        "rwkv_k_k": 0.85 + nrm(ks[9], (L, RWKV_WIDTH), 0.05),
        "rwkv_k_a": 1.0 + nrm(ks[10], (L, RWKV_WIDTH), 0.05),
        "rwkv_r_k": nrm(ks[11], (L, RWKV_WIDTH), 0.1),
        "rwkv_ln_g": 1.0 + nrm(ks[12], (L, RWKV_WIDTH), 0.05),
        "rwkv_ln_b": nrm(ks[13], (L, RWKV_WIDTH), 0.01),
        "p_fox": nrm(ks[14], (L, FOX_WIDTH, D_MODEL), FOX_WIDTH ** -0.5),
        "p_rwkv": nrm(ks[15], (L, RWKV_WIDTH, D_MODEL), RWKV_WIDTH ** -0.5),
        "w_o": nrm(ks[16], (L, D_MODEL, D_MODEL), DN_BETA * D_MODEL ** -0.5),
        "ln1_g": 1.0 + nrm(ks[17], (L, D_MODEL), 0.05),
        "ln1_b": nrm(ks[18], (L, D_MODEL), 0.01),
        "peer_w_q": nrm(ks[19], (L, D_MODEL, PEER_HEADS * 2 * PEER_HALF), D_MODEL ** -0.5),
        "peer_sub_keys": nrm(ks[20], (L, PEER_HEADS, 2, PEER_N_KEYS, PEER_HALF), PEER_HALF ** -0.5),
        "peer_u": nrm(ks[21], (L, PEER_N_EXPERTS, D_MODEL), D_MODEL ** -0.5),
        "peer_v": nrm(ks[22], (L, PEER_N_EXPERTS, D_MODEL), DN_BETA * PEER_HEADS ** -0.5),
        "ln2_g": 1.0 + nrm(ks[23], (L, D_MODEL), 0.05),
        "ln2_b": nrm(ks[24], (L, D_MODEL), 0.01),
    }


def reference(x, w_in, fox_f_bias, rwkv_mu, rwkv_w0, rwkv_w2, rwkv_a0, rwkv_a2, rwkv_g2,
              rwkv_k_k, rwkv_k_a, rwkv_r_k, rwkv_ln_g, rwkv_ln_b, p_fox, p_rwkv, w_o,
              ln1_g, ln1_b, peer_w_q, peer_sub_keys, peer_u, peer_v, ln2_g, ln2_b):
    for l in range(DEPTH):
        p = x @ w_in[l]
        fox_cols = p[..., :FOX_COLS]
        rwkv_cols = _token_shift(p[..., FOX_COLS:FOX_COLS + RWKV_COLS], rwkv_mu[l])
        gate_cols = p[..., FOX_COLS + RWKV_COLS:]
        q, k, v, f_logit = _split(fox_cols, FOX_SPLITS)
        r_r, r_k, r_v, r_wl, r_al, r_gl = _split(rwkv_cols, RWKV_SPLITS)
        gate_fox, gate_rwkv = _split(gate_cols, (D_MODEL, D_MODEL))

        y_fox = _forgetting_attention(q, k, v, f_logit + fox_f_bias[l])
        y_rwkv = _rwkv7_time_mix(r_r, r_k, r_v, r_wl, r_al, r_gl,
                                 rwkv_w0[l], rwkv_w2[l], rwkv_a0[l], rwkv_a2[l], rwkv_g2[l],
                                 rwkv_k_k[l], rwkv_k_a[l], rwkv_r_k[l], rwkv_ln_g[l], rwkv_ln_b[l])
        merged = (jax.nn.sigmoid(gate_fox) * (y_fox @ p_fox[l])
                  + jax.nn.sigmoid(gate_rwkv) * (y_rwkv @ p_rwkv[l]))
        x = _layer_norm(DN_ALPHA * x + merged @ w_o[l], ln1_g[l], ln1_b[l])
        x = _layer_norm(DN_ALPHA * x + _peer(x, peer_w_q[l], peer_sub_keys[l], peer_u[l], peer_v[l]),
                        ln2_g[l], ln2_b[l])
    return x
```

```python
import functools
import math

import jax
import jax.numpy as jnp
from jax import lax
from jax.experimental import pallas as pl
from jax.experimental.pallas import tpu as pltpu

F32 = jnp.float32
BF16 = jnp.bfloat16
I32 = jnp.int32
HI = lax.Precision.HIGHEST

LANES = 128
HEAD_DIM = 64
PEER_TOPK = 16
LN_EPS = 1e-5
GN_EPS = 64e-5
NEG_BIG = -1e30
VMEM_LIMIT = 56 * 1024 * 1024

MM_TM = 512
FG_L = 512
ATT_BQ = 512
ATT_BK = 512
PREP_TM = 256
SCAN_SB = 512
SCAN_C = 64
MERGE_TM = 512
ROUTE_TM = 256
GATHER_TS = 128
GATHER_TB = 8


def _nt(a, b, precision=None):
    return lax.dot_general(a, b, (((1,), (1,)), ((), ())), precision=precision,
                           preferred_element_type=F32)


def _tn(a, b, precision=None):
    return lax.dot_general(a, b, (((0,), (0,)), ((), ())), precision=precision,
                           preferred_element_type=F32)


def _sigmoid(x):
    return 1.0 / (1.0 + jnp.exp(-x))


def _params(*sem):
    return pltpu.CompilerParams(dimension_semantics=sem, vmem_limit_bytes=VMEM_LIMIT)


def _mm_body(x_ref, w_ref, o_ref):
    o_ref[...] = jnp.dot(x_ref[...].astype(BF16), w_ref[...],
                         preferred_element_type=F32).astype(o_ref.dtype)


def _matmul(x, w, out_dtype, tn):
    m, k = x.shape
    n = w.shape[1]
    tm = min(MM_TM, m)
    return pl.pallas_call(
        _mm_body,
        grid=(m // tm, n // tn),
        in_specs=[pl.BlockSpec((tm, k), lambda i, j: (i, 0)),
                  pl.BlockSpec((k, tn), lambda i, j: (0, j))],
        out_specs=pl.BlockSpec((tm, tn), lambda i, j: (i, j)),
        out_shape=jax.ShapeDtypeStruct((m, n), out_dtype),
        compiler_params=_params("parallel", "parallel"),
        name="proj_matmul",
    )(x, w)


def _fgate_body(x_ref, wf_ref, b_ref, c_ref, carry_ref):
    @pl.when(pl.program_id(1) == 0)
    def _():
        carry_ref[...] = jnp.zeros_like(carry_ref)

    xb = x_ref[0].astype(BF16)
    f = _nt(wf_ref[...], xb) + b_ref[...]
    logf = -(jnp.maximum(-f, 0.0) + jnp.log1p(jnp.exp(-jnp.abs(f))))
    n = f.shape[1]
    tri = (lax.broadcasted_iota(I32, (n, n), 0) <= lax.broadcasted_iota(I32, (n, n), 1)).astype(F32)
    cs = jnp.dot(logf, tri, precision=HI, preferred_element_type=F32) + carry_ref[:, 0:1]
    c_ref[0] = cs
    carry_ref[...] = jnp.broadcast_to(cs[:, n - 1:n], carry_ref.shape)


def _forget_cumsum(x3, wf_t, bias):
    b, s, d = x3.shape
    h = wf_t.shape[0]
    blk = min(FG_L, s)
    return pl.pallas_call(
        _fgate_body,
        grid=(b, s // blk),
        in_specs=[pl.BlockSpec((1, blk, d), lambda i, j: (i, j, 0)),
                  pl.BlockSpec((h, d), lambda i, j: (0, 0)),
                  pl.BlockSpec((h, 1), lambda i, j: (0, 0))],
        out_specs=pl.BlockSpec((1, h, blk), lambda i, j: (i, 0, j)),
        out_shape=jax.ShapeDtypeStruct((b, h, s), F32),
        scratch_shapes=[pltpu.VMEM((h, LANES), F32)],
        compiler_params=_params("parallel", "arbitrary"),
        name="forget_cumsum",
    )(x3, wf_t, bias)


def _fox_body(qi_tab, ki_tab, q_ref, k_ref, v_ref, cq_ref, ck_ref, o_ref,
              qh_s, m_s, l_s, acc_s):
    p = pl.program_id(2)
    qi = qi_tab[p]
    ki = ki_tab[p]
    bq = q_ref.shape[1]
    bk = k_ref.shape[1]
    lane = lax.broadcasted_iota(I32, (1, LANES), 1)

    @pl.when(ki == 0)
    def _():
        q = q_ref[0] * jnp.asarray(HEAD_DIM ** -0.5, BF16)
        zero = jnp.zeros_like(q)
        qh_s[0] = jnp.where(lane < HEAD_DIM, q, zero)
        qh_s[1] = jnp.where(lane >= HEAD_DIM, q, zero)
        m_s[...] = jnp.full_like(m_s, NEG_BIG)
        l_s[...] = jnp.zeros_like(l_s)
        acc_s[...] = jnp.zeros_like(acc_s)

    k = k_ref[0]
    v = v_ref[0]
    rows = lax.broadcasted_iota(I32, (bq, bk), 0) + qi * bq
    cols = lax.broadcasted_iota(I32, (bq, bk), 1) + ki * bk
    causal = rows >= cols
    for h in range(2):
        s = _nt(qh_s[h], k)
        s = s + cq_ref[0, 0][:, h:h + 1] - ck_ref[0, 0][h:h + 1, :]
        s = jnp.where(causal, s, NEG_BIG)
        m_prev = m_s[h]
        m_new = jnp.maximum(m_prev, jnp.max(s, axis=1, keepdims=True))
        alpha = jnp.exp(m_prev - m_new)
        pr = jnp.exp(s - m_new)
        l_s[h] = alpha * l_s[h] + jnp.sum(pr, axis=1, keepdims=True)
        acc_s[h] = alpha * acc_s[h] + jnp.dot(pr.astype(BF16), v, preferred_element_type=F32)
        m_s[h] = m_new

    @pl.when(ki == qi)
    def _():
        o0 = acc_s[0] / l_s[0]
        o1 = acc_s[1] / l_s[1]
        o_ref[0] = jnp.where(lane < HEAD_DIM, o0, o1).astype(o_ref.dtype)


def _fox_attention(qkv, c_col, c_row, b, s, width):
    hp = width // LANES
    bq = min(ATT_BQ, s)
    nq = s // bq
    pairs = [(i, j) for i in range(nq) for j in range(i + 1)]
    qi_tab = jnp.asarray([pq for pq, _ in pairs], I32)
    ki_tab = jnp.asarray([pk for _, pk in pairs], I32)
    grid_spec = pltpu.PrefetchScalarGridSpec(
        num_scalar_prefetch=2,
        grid=(b, hp, len(pairs)),
        in_specs=[
            pl.BlockSpec((1, bq, LANES), lambda i, j, p, qt, kt: (i, qt[p], j)),
            pl.BlockSpec((1, bq, LANES), lambda i, j, p, qt, kt: (i, kt[p], hp + j)),
            pl.BlockSpec((1, bq, LANES), lambda i, j, p, qt, kt: (i, kt[p], 2 * hp + j)),
            pl.BlockSpec((1, 1, bq, 2), lambda i, j, p, qt, kt: (i, j, qt[p], 0)),
            pl.BlockSpec((1, 1, 2, bq), lambda i, j, p, qt, kt: (i, j, 0, kt[p])),
        ],
        out_specs=pl.BlockSpec((1, bq, LANES), lambda i, j, p, qt, kt: (i, qt[p], j)),
        scratch_shapes=[pltpu.VMEM((2, bq, LANES), BF16),
                        pltpu.VMEM((2, bq, 1), F32),
                        pltpu.VMEM((2, bq, 1), F32),
                        pltpu.VMEM((2, bq, LANES), F32)],
    )
    return pl.pallas_call(
        _fox_body,
        grid_spec=grid_spec,
        out_shape=jax.ShapeDtypeStruct((b, s, width), BF16),
        compiler_params=_params("parallel", "parallel", "arbitrary"),
        name="fox_attention",
    )(qi_tab, ki_tab, qkv, qkv, qkv, c_col, c_row)


def _prep_body(p_ref, mu_ref, w0_ref, a0_ref, kk_ref, ka_ref, rk_ref, w2_ref, a2_ref, g2_ref,
               e_ref, r_o, lw_o, k_o, v_o, kk_o, b_o, bonus_o, g_o, carry_ref):
    width = r_o.shape[2]
    p = p_ref[0]
    tm = p.shape[0]

    @pl.when(pl.program_id(1) == 0)
    def _():
        carry_ref[...] = jnp.zeros_like(carry_ref)

    row = lax.broadcasted_iota(I32, p.shape, 0)
    prev = jnp.where(row == 0, jnp.broadcast_to(carry_ref[0:1, :], p.shape), pltpu.roll(p, 1, 0))
    carry_ref[...] = jnp.broadcast_to(p[tm - 1:tm, :], carry_ref.shape)
    xs = p + mu_ref[...] * (prev - p)

    r = xs[:, 0:width]
    k = xs[:, width:2 * width]
    v = xs[:, 2 * width:3 * width]
    wl = xs[:, 3 * width:3 * width + LANES]
    al = xs[:, 3 * width + LANES:3 * width + 2 * LANES]
    gl = xs[:, 3 * width + 2 * LANES:3 * width + 3 * LANES]

    w_pre = w0_ref[...] + jnp.dot(jnp.tanh(wl), w2_ref[...], precision=HI, preferred_element_type=F32)
    lw = (-math.exp(-0.5)) * _sigmoid(w_pre)
    a = _sigmoid(a0_ref[...] + jnp.dot(al, a2_ref[...], precision=HI, preferred_element_type=F32))
    g = jnp.dot(_sigmoid(gl), g2_ref[...], precision=HI, preferred_element_type=F32)
    kk = k * kk_ref[...]
    ss = jnp.dot(kk * kk, e_ref[...], precision=HI, preferred_element_type=F32)
    kkn = kk / jnp.maximum(jnp.sqrt(ss), 1e-12)
    kmod = k * (1.0 + (a - 1.0) * ka_ref[...])
    bonus = jnp.dot(r * kmod * rk_ref[...], e_ref[...], precision=HI, preferred_element_type=F32) * v

    r_o[0] = r
    lw_o[0] = lw
    k_o[0] = kmod
    v_o[0] = v
    kk_o[0] = kkn
    b_o[0] = kkn * a
    bonus_o[0] = bonus
    g_o[0] = g


def _rwkv_prep(rw3, mu, w0, a0, k_k, k_a, r_k, w2p, a2p, g2, e_mat):
    b, s, cols = rw3.shape
    width = w0.shape[1]
    tm = min(PREP_TM, s)
    vec = lambda n: pl.BlockSpec((1, n), lambda i, j: (0, 0))
    mat = lambda a: pl.BlockSpec(a.shape, lambda i, j: (0, 0))
    out = pl.BlockSpec((1, tm, width), lambda i, j: (i, j, 0))
    return pl.pallas_call(
        _prep_body,
        grid=(b, s // tm),
        in_specs=[pl.BlockSpec((1, tm, cols), lambda i, j: (i, j, 0)),
                  vec(cols), vec(width), vec(width), vec(width), vec(width), vec(width),
                  mat(w2p), mat(a2p), mat(g2), mat(e_mat)],
        out_specs=[out] * 8,
        out_shape=[jax.ShapeDtypeStruct((b, s, width), F32)] * 8,
        scratch_shapes=[pltpu.VMEM((8, cols), F32)],
        compiler_params=_params("parallel", "arbitrary"),
        name="rwkv_prep",
    )(rw3, mu, w0, a0, k_k, k_a, r_k, w2p, a2p, g2, e_mat)


def _scan_body(r_ref, lw_ref, k_ref, v_ref, kk_ref, b_ref, bonus_ref, g_ref, lng_ref, lnb_ref,
               y_ref, s_ref):
    sb = r_ref.shape[1]
    c = min(SCAN_C, sb)

    @pl.when(pl.program_id(2) == 0)
    def _():
        s_ref[...] = jnp.zeros_like(s_ref)

    lane = lax.broadcasted_iota(I32, (1, LANES), 1)
    head0 = lane < HEAD_DIM
    ti = lax.broadcasted_iota(I32, (c, c), 0)
    si = lax.broadcasted_iota(I32, (c, c), 1)
    incl = ti >= si
    strict = ti > si
    ltri = incl.astype(F32)
    eye = (ti == si).astype(F32)
    ri = lax.broadcasted_iota(I32, (LANES, LANES), 0)
    ci = lax.broadcasted_iota(I32, (LANES, LANES), 1)
    same_head = (ri < HEAD_DIM) == (ci < HEAD_DIM)
    seg_mean = same_head.astype(F32) * (1.0 / HEAD_DIM)
    dot = functools.partial(jnp.dot, precision=HI, preferred_element_type=F32)

    for ch in range(sb // c):
        sl = pl.ds(ch * c, c)
        r = r_ref[0, sl, :]
        lw = lw_ref[0, sl, :]
        k = k_ref[0, sl, :]
        v = v_ref[0, sl, :]
        kk = kk_ref[0, sl, :]
        bb = b_ref[0, sl, :]

        cum = dot(ltri, lw)
        g_in = jnp.exp(cum)
        g_ex = jnp.exp(cum - lw)
        g_inv = jnp.exp(-cum)
        a_m = kk * g_ex
        b_m = bb * g_inv
        k_m = k * g_inv
        r_m = r * g_in

        tinv, lak, mrb, mrk = [], [], [], []
        for h in range(2):
            msk = head0 if h == 0 else jnp.logical_not(head0)
            a_h = jnp.where(msk, a_m, 0.0)
            r_h = jnp.where(msk, r_m, 0.0)
            lab = jnp.where(strict, _nt(a_h, b_m, HI), 0.0)
            lak.append(jnp.where(strict, _nt(a_h, k_m, HI), 0.0))
            mrb.append(jnp.where(incl, _nt(r_h, b_m, HI), 0.0))
            mrk.append(jnp.where(incl, _nt(r_h, k_m, HI), 0.0))
            pw = -lab
            t = eye + pw
            steps = max(1, int(math.ceil(math.log2(c))) - 1)
            for _ in range(steps):
                pw = dot(pw, pw)
                t = t + dot(t, pw)
            tinv.append(t)

        st = s_ref[...]
        x_m = -_nt(a_m, st, HI) - jnp.where(head0, dot(lak[0], v), dot(lak[1], v))
        u = jnp.where(head0, dot(tinv[0], x_m), dot(tinv[1], x_m))
        y = (_nt(r_m, st, HI)
             + jnp.where(head0, dot(mrb[0], u) + dot(mrk[0], v), dot(mrb[1], u) + dot(mrk[1], v)))
        upd = _tn(u, b_m, HI) + _tn(v, k_m, HI)
        s_ref[...] = (st + jnp.where(same_head, upd, 0.0)) * g_in[c - 1:c, :]

        mu = dot(y, seg_mean)
        yc = y - mu
        var = dot(yc * yc, seg_mean)
        yn = yc * lax.rsqrt(var + GN_EPS) * lng_ref[...] + lnb_ref[...]
        y_ref[0, sl, :] = ((yn + bonus_ref[0, sl, :]) * g_ref[0, sl, :]).astype(y_ref.dtype)


def _rwkv_scan(r, lw, k, v, kk, bb, bonus, g, ln_g, ln_b):
    b, s, width = r.shape
    hp = width // LANES
    sb = min(SCAN_SB, s)
    blk = pl.BlockSpec((1, sb, LANES), lambda i, j, t: (i, t, j))
    vec = pl.BlockSpec((1, LANES), lambda i, j, t: (0, j))
    return pl.pallas_call(
        _scan_body,
        grid=(b, hp, s // sb),
        in_specs=[blk] * 8 + [vec, vec],
        out_specs=blk,
        out_shape=jax.ShapeDtypeStruct((b, s, width), BF16),
        scratch_shapes=[pltpu.VMEM((LANES, LANES), F32)],
        compiler_params=_params("parallel", "parallel", "arbitrary"),
        name="rwkv_scan",
    )(r, lw, k, v, kk, bb, bonus, g, ln_g, ln_b)


def _layer_norm(y, g, b):
    mu = jnp.mean(y, axis=-1, keepdims=True)
    yc = y - mu
    var = jnp.mean(yc * yc, axis=-1, keepdims=True)
    return yc * lax.rsqrt(var + LN_EPS) * g + b


def _merge_body(alpha, x_ref, yf_ref, yr_ref, gf_ref, gr_ref, pf_ref, pr_ref, wo_ref, g_ref, b_ref, o_ref):
    mf = jnp.dot(yf_ref[...], pf_ref[...], preferred_element_type=F32)
    mr = jnp.dot(yr_ref[...], pr_ref[...], preferred_element_type=F32)
    merged = _sigmoid(gf_ref[...]) * mf + _sigmoid(gr_ref[...]) * mr
    out = jnp.dot(merged.astype(BF16), wo_ref[...], preferred_element_type=F32)
    o_ref[...] = _layer_norm(alpha * x_ref[...] + out, g_ref[...], b_ref[...])


def _merge(alpha, x2, y_fox, y_rwkv, gates, p_fox, p_rwkv, w_o, ln_g, ln_b):
    n, d = x2.shape
    wf = y_fox.shape[1]
    wr = y_rwkv.shape[1]
    tm = min(MERGE_TM, n)
    row = lambda w, c=0: pl.BlockSpec((tm, w), lambda i: (i, c))
    full = lambda a: pl.BlockSpec(a.shape, lambda i: (0, 0))
    return pl.pallas_call(
        functools.partial(_merge_body, alpha),
        grid=(n // tm,),
        in_specs=[row(d), row(wf), row(wr), row(d, 0), row(d, 1),
                  full(p_fox), full(p_rwkv), full(w_o), full(ln_g), full(ln_b)],
        out_specs=row(d),
        out_shape=jax.ShapeDtypeStruct((n, d), F32),
        compiler_params=_params("parallel"),
        name="merge_ln",
    )(x2, y_fox, y_rwkv, gates, gates, p_fox, p_rwkv, w_o, ln_g, ln_b)


def _topk_rows(vals, k, payload=None):
    n, t = vals.shape
    rio = lax.broadcasted_iota(I32, (n, t), 0)
    kio = lax.broadcasted_iota(I32, (k, t), 0)
    out_v = jnp.zeros((k, t), F32)
    out_i = jnp.zeros((k, t), I32)
    cur = vals
    for it in range(k):
        mx = jnp.max(cur, axis=0, keepdims=True)
        sel = jnp.min(jnp.where(cur == mx, rio, n), axis=0, keepdims=True)
        hit = rio == sel
        if payload is None:
            got = sel
        else:
            got = jnp.sum(jnp.where(hit, payload, 0), axis=0, keepdims=True)
        out_v = jnp.where(kio == it, mx, out_v)
        out_i = jnp.where(kio == it, got, out_i)
        cur = jnp.where(hit, -jnp.inf, cur)
    return out_v, out_i


def _route_body(n_keys, x_ref, wq_ref, keys_ref, idx_ref, gate_ref):
    heads = keys_ref.shape[0] // 2
    half = keys_ref.shape[2]
    kt = PEER_TOPK
    q = jnp.dot(x_ref[...].astype(BF16), wq_ref[...], preferred_element_type=F32)
    idx_rows, gate_rows = [], []
    for h in range(heads):
        tops = []
        for p in range(2):
            c = 2 * h + p
            qc = q[:, c * half:(c + 1) * half].astype(BF16)
            s_t = _nt(keys_ref[c], qc)
            tops.append(_topk_rows(s_t, kt))
        (s0, i0), (s1, i1) = tops
        cand_s = jnp.concatenate([s0[a:a + 1, :] + s1 for a in range(kt)], axis=0)
        cand_i = jnp.concatenate([i0[a:a + 1, :] * n_keys + i1 for a in range(kt)], axis=0)
        best_s, best_i = _topk_rows(cand_s, kt, cand_i)
        e = jnp.exp(best_s - best_s[0:1, :])
        gate_rows.append(e / jnp.sum(e, axis=0, keepdims=True))
        idx_rows.append(best_i)
    idx_ref[...] = jnp.concatenate(idx_rows, axis=0).T
    gate_ref[...] = jnp.concatenate(gate_rows, axis=0).T


def _peer_route(x1, w_q, keys):
    n, d = x1.shape
    n_keys = keys.shape[1]
    slots = (keys.shape[0] // 2) * PEER_TOPK
    tm = min(ROUTE_TM, n)
    return pl.pallas_call(
        functools.partial(_route_body, n_keys),
        grid=(n // tm,),
        in_specs=[pl.BlockSpec((tm, d), lambda i: (i, 0)),
                  pl.BlockSpec(w_q.shape, lambda i: (0, 0)),
                  pl.BlockSpec(keys.shape, lambda i: (0, 0, 0))],
        out_specs=[pl.BlockSpec((tm, slots), lambda i: (i, 0))] * 2,
        out_shape=[jax.ShapeDtypeStruct((n, slots), I32), jax.ShapeDtypeStruct((n, slots), F32)],
        compiler_params=_params("parallel"),
        name="peer_route",
    )(x1, w_q, keys)


def _gelu(x):
    return 0.5 * x * (1.0 + lax.erf(x * (2.0 ** -0.5)))


def _gather_body(alpha, x_ref, gate_ref, idx_hbm, u_hbm, v_hbm, g_ref, b_ref, o_ref,
                 idx_s, ubuf, vbuf, sem_i, sem_u, sem_v):
    ts, slots = gate_ref.shape
    tb = GATHER_TB
    rows = tb * slots
    nsub = ts // tb
    step = pl.program_id(0)

    cp_i = pltpu.make_async_copy(idx_hbm.at[pl.ds(step * ts, ts), :], idx_s, sem_i)
    cp_i.start()
    cp_i.wait()

    def issue(j, slot):
        def one(rr, carry):
            e = idx_s[j * tb + rr // slots, rr % slots]
            pltpu.make_async_copy(u_hbm.at[pl.ds(e, 1), :], ubuf.at[slot, pl.ds(rr, 1), :],
                                  sem_u.at[slot]).start()
            pltpu.make_async_copy(v_hbm.at[pl.ds(e, 1), :], vbuf.at[slot, pl.ds(rr, 1), :],
                                  sem_v.at[slot]).start()
            return carry
        lax.fori_loop(0, rows, one, 0, unroll=8)

    def wait(slot):
        pltpu.make_async_copy(u_hbm.at[pl.ds(0, rows), :], ubuf.at[slot], sem_u.at[slot]).wait()
        pltpu.make_async_copy(v_hbm.at[pl.ds(0, rows), :], vbuf.at[slot], sem_v.at[slot]).wait()

    eye = (lax.broadcasted_iota(I32, (slots, slots), 0)
           == lax.broadcasted_iota(I32, (slots, slots), 1)).astype(F32)
    rio = lax.broadcasted_iota(I32, (tb, 1), 0)

    issue(0, 0)

    def sub(j, carry):
        slot = lax.rem(j, 2)

        @pl.when(j + 1 < nsub)
        def _():
            issue(j + 1, 1 - slot)

        wait(slot)
        r0 = pl.multiple_of(j * tb, tb)
        xb = x_ref[pl.ds(r0, tb), :]
        gcol = _nt(eye, gate_ref[pl.ds(r0, tb), :], HI)
        acc = jnp.zeros(xb.shape, F32)
        for t in range(tb):
            xt = xb[t:t + 1, :]
            h = jnp.sum(ubuf[slot, pl.ds(t * slots, slots), :] * xt, axis=1, keepdims=True)
            w = _gelu(h) * gcol[:, t:t + 1]
            o = jnp.sum(vbuf[slot, pl.ds(t * slots, slots), :] * w, axis=0, keepdims=True)
            acc = jnp.where(rio == t, o, acc)
        o_ref[pl.ds(r0, tb), :] = _layer_norm(alpha * xb + acc, g_ref[...], b_ref[...])
        return carry

    lax.fori_loop(0, nsub, sub, 0)


def _peer_gather(alpha, x1, gates, idx, u_tab, v_tab, ln_g, ln_b):
    n, d = x1.shape
    slots = gates.shape[1]
    ts = min(GATHER_TS, n)
    rows = GATHER_TB * slots
    any_spec = pl.BlockSpec(memory_space=pl.ANY)
    return pl.pallas_call(
        functools.partial(_gather_body, alpha),
        grid=(n // ts,),
        in_specs=[pl.BlockSpec((ts, d), lambda i: (i, 0)),
                  pl.BlockSpec((ts, slots), lambda i: (i, 0)),
                  any_spec, any_spec, any_spec,
                  pl.BlockSpec((1, d), lambda i: (0, 0)),
                  pl.BlockSpec((1, d), lambda i: (0, 0))],
        out_specs=pl.BlockSpec((ts, d), lambda i: (i, 0)),
        out_shape=jax.ShapeDtypeStruct((n, d), F32),
        scratch_shapes=[pltpu.SMEM((ts, slots), I32),
                        pltpu.VMEM((2, rows, d), F32),
                        pltpu.VMEM((2, rows, d), F32),
                        pltpu.SemaphoreType.DMA(()),
                        pltpu.SemaphoreType.DMA((2,)),
                        pltpu.SemaphoreType.DMA((2,))],
        compiler_params=_params("arbitrary"),
        name="peer_gather",
    )(x1, gates, idx, u_tab, v_tab, ln_g, ln_b)


def _pad_rows(w, n):
    return jnp.pad(w, ((0, n - w.shape[0]), (0, 0)))


def kernel(x, w_in, fox_f_bias, rwkv_mu, rwkv_w0, rwkv_w2, rwkv_a0, rwkv_a2, rwkv_g2, rwkv_k_k, rwkv_k_a, rwkv_r_k, rwkv_ln_g, rwkv_ln_b, p_fox, p_rwkv, w_o, ln1_g, ln1_b, peer_w_q, peer_sub_keys, peer_u, peer_v, ln2_g, ln2_b):
    bsz, seq, d = x.shape
    depth = w_in.shape[0]
    n = bsz * seq
    fox_heads = fox_f_bias.shape[1]
    fw = p_fox.shape[1]
    rw = p_rwkv.shape[1]
    w_lora, a_lora, g_lora = rwkv_w2.shape[1], rwkv_a2.shape[1], rwkv_g2.shape[1]
    fox_cols = 3 * fw + fox_heads
    rwkv_cols = 3 * rw + w_lora + a_lora + g_lora
    alpha = (2 * depth) ** 0.25
    hp = fw // LANES

    seg = (jnp.arange(rw)[:, None] // HEAD_DIM == jnp.arange(rw)[None, :] // HEAD_DIM).astype(F32)

    xcur = x.reshape(n, d)
    for l in range(depth):
        w = w_in[l]
        w_qkv = w[:, :3 * fw].astype(BF16)
        wf_t = w[:, 3 * fw:fox_cols].T.astype(BF16)
        wr = w[:, fox_cols:fox_cols + rwkv_cols]
        zpad = lambda m, c: jnp.pad(m, ((0, 0), (0, c - m.shape[1])))
        o_w = 3 * rw
        w_rw = jnp.concatenate([wr[:, :o_w],
                                zpad(wr[:, o_w:o_w + w_lora], LANES),
                                zpad(wr[:, o_w + w_lora:o_w + w_lora + a_lora], LANES),
                                zpad(wr[:, o_w + w_lora + a_lora:], LANES)], axis=1).astype(BF16)
        mu = rwkv_mu[l]
        mu_p = jnp.concatenate([mu[:o_w],
                                jnp.pad(mu[o_w:o_w + w_lora], (0, LANES - w_lora)),
                                jnp.pad(mu[o_w + w_lora:o_w + w_lora + a_lora], (0, LANES - a_lora)),
                                jnp.pad(mu[o_w + w_lora + a_lora:], (0, LANES - g_lora))])[None, :]
        w_gate = w[:, fox_cols + rwkv_cols:].astype(BF16)

        qkv = _matmul(xcur, w_qkv, BF16, 512)
        rwp = _matmul(xcur, w_rw, F32, 640)
        gates = _matmul(xcur, w_gate, F32, 512)

        c_row = _forget_cumsum(xcur.reshape(bsz, seq, d), wf_t, fox_f_bias[l][:, None])
        c_row = c_row.reshape(bsz, hp, 2, seq)
        c_col = jnp.swapaxes(c_row, 2, 3)
        y_fox = _fox_attention(qkv.reshape(bsz, seq, 3 * fw), c_col, c_row, bsz, seq, fw)

        row = lambda a: a[l][None, :]
        r_, lw_, k_, v_, kk_, b_, bonus_, g_ = _rwkv_prep(
            rwp.reshape(bsz, seq, -1), mu_p, row(rwkv_w0), row(rwkv_a0), row(rwkv_k_k), row(rwkv_k_a),
            row(rwkv_r_k), _pad_rows(rwkv_w2[l], LANES), _pad_rows(rwkv_a2[l], LANES), rwkv_g2[l], seg)
        y_rwkv = _rwkv_scan(r_, lw_, k_, v_, kk_, b_, bonus_, g_, row(rwkv_ln_g), row(rwkv_ln_b))

        x1 = _merge(alpha, xcur, y_fox.reshape(n, fw), y_rwkv.reshape(n, rw), gates,
                    p_fox[l].astype(BF16), p_rwkv[l].astype(BF16), w_o[l].astype(BF16),
                    row(ln1_g), row(ln1_b))

        keys = peer_sub_keys[l]
        n_keys, half = keys.shape[2], keys.shape[3]
        idx, pg = _peer_route(x1, peer_w_q[l].astype(BF16),
                              keys.reshape(-1, n_keys, half).astype(BF16))
        xcur = _peer_gather(alpha, x1, pg, idx, peer_u[l], peer_v[l], row(ln2_g), row(ln2_b))
    return xcur.reshape(bsz, seq, d)
```

```python
import functools
import math

import jax
import jax.numpy as jnp
from jax import lax
from jax.experimental import pallas as pl
from jax.experimental.pallas import tpu as pltpu

F32 = jnp.float32
BF16 = jnp.bfloat16
I32 = jnp.int32
HI = lax.Precision.HIGHEST

LANES = 128
HEAD_DIM = 64
PEER_TOPK = 16
LN_EPS = 1e-5
GN_EPS = 64e-5
NEG_BIG = -1e30
VMEM_LIMIT = 56 * 1024 * 1024

MM_TM = 512
FG_L = 512
ATT_BQ = 1024
PREP_TM = 256
SCAN_SB = 512
SCAN_C = 64
MERGE_TM = 512
ROUTE_TM = 256
GATHER_TS = 128
GATHER_TB = 8
GATHER_PITCH = 20


def _nt(a, b, precision=None):
    return lax.dot_general(a, b, (((1,), (1,)), ((), ())), precision=precision,
                           preferred_element_type=F32)


def _tn(a, b, precision=None):
    return lax.dot_general(a, b, (((0,), (0,)), ((), ())), precision=precision,
                           preferred_element_type=F32)


_NN = ((1,), (0,))
_NT = ((1,), (1,))
_TN = ((0,), (0,))


def _bdot(a, b, dims):
    return lax.dot_general(a.astype(BF16), b.astype(BF16), (dims, ((), ())), preferred_element_type=F32)


def _sigmoid(x):
    return 1.0 / (1.0 + jnp.exp(-x))


def _params(*sem):
    return pltpu.CompilerParams(dimension_semantics=sem, vmem_limit_bytes=VMEM_LIMIT)


def _mm_body(x_ref, w_ref, o_ref):
    o_ref[...] = jnp.dot(x_ref[...].astype(BF16), w_ref[...],
                         preferred_element_type=F32).astype(o_ref.dtype)


def _matmul(x, w, out_dtype, tn):
    m, k = x.shape
    n = w.shape[1]
    tm = min(MM_TM, m)
    return pl.pallas_call(
        _mm_body,
        grid=(m // tm, n // tn),
        in_specs=[pl.BlockSpec((tm, k), lambda i, j: (i, 0)),
                  pl.BlockSpec((k, tn), lambda i, j: (0, j))],
        out_specs=pl.BlockSpec((tm, tn), lambda i, j: (i, j)),
        out_shape=jax.ShapeDtypeStruct((m, n), out_dtype),
        compiler_params=_params("parallel", "parallel"),
        name="proj_matmul",
    )(x, w)


def _fgate_body(x_ref, wf_ref, b_ref, c_ref, carry_ref):
    @pl.when(pl.program_id(1) == 0)
    def _():
        carry_ref[...] = jnp.zeros_like(carry_ref)

    xb = x_ref[0].astype(BF16)
    f = _nt(wf_ref[...], xb) + b_ref[...]
    logf = -(jnp.maximum(-f, 0.0) + jnp.log1p(jnp.exp(-jnp.abs(f))))
    n = f.shape[1]
    tri = (lax.broadcasted_iota(I32, (n, n), 0) <= lax.broadcasted_iota(I32, (n, n), 1)).astype(F32)
    cs = jnp.dot(logf, tri, precision=HI, preferred_element_type=F32) + carry_ref[:, 0:1]
    c_ref[0] = cs
    carry_ref[...] = jnp.broadcast_to(cs[:, n - 1:n], carry_ref.shape)


def _forget_cumsum(x3, wf_t, bias):
    b, s, d = x3.shape
    h = wf_t.shape[0]
    blk = min(FG_L, s)
    return pl.pallas_call(
        _fgate_body,
        grid=(b, s // blk),
        in_specs=[pl.BlockSpec((1, blk, d), lambda i, j: (i, j, 0)),
                  pl.BlockSpec((h, d), lambda i, j: (0, 0)),
                  pl.BlockSpec((h, 1), lambda i, j: (0, 0))],
        out_specs=pl.BlockSpec((1, h, blk), lambda i, j: (i, 0, j)),
        out_shape=jax.ShapeDtypeStruct((b, h, s), F32),
        scratch_shapes=[pltpu.VMEM((h, LANES), F32)],
        compiler_params=_params("parallel", "arbitrary"),
        name="forget_cumsum",
    )(x3, wf_t, bias)


def _fox_body(qi_tab, ki_tab, q_ref, k_ref, v_ref, cq_ref, ck_ref, o_ref,
              qh_s, m_s, l_s, acc_s):
    p = pl.program_id(2)
    qi = qi_tab[p]
    ki = ki_tab[p]
    bq = q_ref.shape[1]
    bk = k_ref.shape[1]
    lane = lax.broadcasted_iota(I32, (1, LANES), 1)

    @pl.when(ki == 0)
    def _():
        q = q_ref[0] * jnp.asarray(HEAD_DIM ** -0.5, BF16)
        zero = jnp.zeros_like(q)
        qh_s[0] = jnp.where(lane < HEAD_DIM, q, zero)
        qh_s[1] = jnp.where(lane >= HEAD_DIM, q, zero)
        m_s[...] = jnp.full_like(m_s, NEG_BIG)
        l_s[...] = jnp.zeros_like(l_s)
        acc_s[...] = jnp.zeros_like(acc_s)

    def step(masked):
        k = k_ref[0]
        v = v_ref[0]
        for h in range(2):
            cq = cq_ref[0, 0][:, h:h + 1]
            z = _nt(qh_s[h], k) - ck_ref[0, 0][h:h + 1, :]
            if masked:
                rows = lax.broadcasted_iota(I32, (bq, bk), 0)
                cols = lax.broadcasted_iota(I32, (bq, bk), 1)
                z = jnp.where(rows >= cols, z, NEG_BIG)
            m_prev = m_s[h]
            m_new = jnp.maximum(m_prev, jnp.max(z, axis=1, keepdims=True) + cq)
            alpha = jnp.exp(m_prev - m_new)
            pr = jnp.exp(z - (m_new - cq))
            l_s[h] = alpha * l_s[h] + jnp.sum(pr, axis=1, keepdims=True)
            acc_s[h] = alpha * acc_s[h] + jnp.dot(pr.astype(BF16), v, preferred_element_type=F32)
            m_s[h] = m_new

    @pl.when(ki < qi)
    def _():
        step(False)

    @pl.when(ki == qi)
    def _():
        step(True)
        o0 = acc_s[0] / l_s[0]
        o1 = acc_s[1] / l_s[1]
        o_ref[0] = jnp.where(lane < HEAD_DIM, o0, o1).astype(o_ref.dtype)


def _fox_attention(qkv, c_col, c_row, b, s, width):
    hp = width // LANES
    bq = min(ATT_BQ, s)
    nq = s // bq
    pairs = [(i, j) for i in range(nq) for j in range(i + 1)]
    qi_tab = jnp.asarray([pq for pq, _ in pairs], I32)
    ki_tab = jnp.asarray([pk for _, pk in pairs], I32)
    grid_spec = pltpu.PrefetchScalarGridSpec(
        num_scalar_prefetch=2,
        grid=(b, hp, len(pairs)),
        in_specs=[
            pl.BlockSpec((1, bq, LANES), lambda i, j, p, qt, kt: (i, qt[p], j)),
            pl.BlockSpec((1, bq, LANES), lambda i, j, p, qt, kt: (i, kt[p], hp + j)),
            pl.BlockSpec((1, bq, LANES), lambda i, j, p, qt, kt: (i, kt[p], 2 * hp + j)),
            pl.BlockSpec((1, 1, bq, 2), lambda i, j, p, qt, kt: (i, j, qt[p], 0)),
            pl.BlockSpec((1, 1, 2, bq), lambda i, j, p, qt, kt: (i, j, 0, kt[p])),
        ],
        out_specs=pl.BlockSpec((1, bq, LANES), lambda i, j, p, qt, kt: (i, qt[p], j)),
        scratch_shapes=[pltpu.VMEM((2, bq, LANES), BF16),
                        pltpu.VMEM((2, bq, 1), F32),
                        pltpu.VMEM((2, bq, 1), F32),
                        pltpu.VMEM((2, bq, LANES), F32)],
    )
    return pl.pallas_call(
        _fox_body,
        grid_spec=grid_spec,
        out_shape=jax.ShapeDtypeStruct((b, s, width), BF16),
        compiler_params=_params("parallel", "parallel", "arbitrary"),
        name="fox_attention",
    )(qi_tab, ki_tab, qkv, qkv, qkv, c_col, c_row)


def _prep_body(p_ref, mu_ref, w0_ref, a0_ref, kk_ref, ka_ref, rk_ref, w2_ref, a2_ref, g2_ref,
               e_ref, r_o, lw_o, k_o, v_o, kk_o, b_o, bonus_o, g_o, carry_ref):
    width = r_o.shape[2]
    p = p_ref[0]
    tm = p.shape[0]

    @pl.when(pl.program_id(1) == 0)
    def _():
        carry_ref[...] = jnp.zeros_like(carry_ref)

    row = lax.broadcasted_iota(I32, p.shape, 0)
    prev = jnp.where(row == 0, jnp.broadcast_to(carry_ref[0:1, :], p.shape), pltpu.roll(p, 1, 0))
    carry_ref[...] = jnp.broadcast_to(p[tm - 1:tm, :], carry_ref.shape)
    xs = p + mu_ref[...] * (prev - p)

    r = xs[:, 0:width]
    k = xs[:, width:2 * width]
    v = xs[:, 2 * width:3 * width]
    wl = xs[:, 3 * width:3 * width + LANES]
    al = xs[:, 3 * width + LANES:3 * width + 2 * LANES]
    gl = xs[:, 3 * width + 2 * LANES:3 * width + 3 * LANES]

    w_pre = w0_ref[...] + jnp.dot(jnp.tanh(wl), w2_ref[...], precision=HI, preferred_element_type=F32)
    lw = (-math.exp(-0.5)) * _sigmoid(w_pre)
    a = _sigmoid(a0_ref[...] + jnp.dot(al, a2_ref[...], precision=HI, preferred_element_type=F32))
    g = jnp.dot(_sigmoid(gl), g2_ref[...], precision=HI, preferred_element_type=F32)
    kk = k * kk_ref[...]
    ss = jnp.dot(kk * kk, e_ref[...], precision=HI, preferred_element_type=F32)
    kkn = kk / jnp.maximum(jnp.sqrt(ss), 1e-12)
    kmod = k * (1.0 + (a - 1.0) * ka_ref[...])
    bonus = jnp.dot(r * kmod * rk_ref[...], e_ref[...], precision=HI, preferred_element_type=F32) * v

    r_o[0] = r
    lw_o[0] = lw
    k_o[0] = kmod
    v_o[0] = v
    kk_o[0] = kkn
    b_o[0] = kkn * a
    bonus_o[0] = bonus
    g_o[0] = g


def _rwkv_prep(rw3, mu, w0, a0, k_k, k_a, r_k, w2p, a2p, g2, e_mat):
    b, s, cols = rw3.shape
    width = w0.shape[1]
    tm = min(PREP_TM, s)
    vec = lambda n: pl.BlockSpec((1, n), lambda i, j: (0, 0))
    mat = lambda a: pl.BlockSpec(a.shape, lambda i, j: (0, 0))
    out = pl.BlockSpec((1, tm, width), lambda i, j: (i, j, 0))
    return pl.pallas_call(
        _prep_body,
        grid=(b, s // tm),
        in_specs=[pl.BlockSpec((1, tm, cols), lambda i, j: (i, j, 0)),
                  vec(cols), vec(width), vec(width), vec(width), vec(width), vec(width),
                  mat(w2p), mat(a2p), mat(g2), mat(e_mat)],
        out_specs=[out] * 8,
        out_shape=[jax.ShapeDtypeStruct((b, s, width), F32)] * 8,
        scratch_shapes=[pltpu.VMEM((8, cols), F32)],
        compiler_params=_params("parallel", "arbitrary"),
        name="rwkv_prep",
    )(rw3, mu, w0, a0, k_k, k_a, r_k, w2p, a2p, g2, e_mat)


def _scan_body(r_ref, lw_ref, k_ref, v_ref, kk_ref, b_ref, bonus_ref, g_ref, lng_ref, lnb_ref,
               y_ref, s_ref):
    sb = r_ref.shape[1]
    c = min(SCAN_C, sb)

    @pl.when(pl.program_id(2) == 0)
    def _():
        s_ref[...] = jnp.zeros_like(s_ref)

    lane = lax.broadcasted_iota(I32, (1, LANES), 1)
    head0 = lane < HEAD_DIM
    ti = lax.broadcasted_iota(I32, (c, c), 0)
    si = lax.broadcasted_iota(I32, (c, c), 1)
    incl = ti >= si
    strict = ti > si
    ltri = incl.astype(BF16)
    eye = (ti == si).astype(F32)
    ri = lax.broadcasted_iota(I32, (LANES, LANES), 0)
    ci = lax.broadcasted_iota(I32, (LANES, LANES), 1)
    same_head = (ri < HEAD_DIM) == (ci < HEAD_DIM)
    seg_mean = same_head.astype(F32) * (1.0 / HEAD_DIM)
    nchunk = sb // c
    sel = lambda x0, x1: jnp.where(head0, x0, x1)


    am, bm, km, rm, vv, glast = [], [], [], [], [], []
    for ch in range(nchunk):
        sl = pl.ds(ch * c, c)
        lw = lw_ref[0, sl, :]
        l1 = lw.astype(BF16)
        r1 = lw - l1.astype(F32)
        l2 = r1.astype(BF16)
        l3 = (r1 - l2.astype(F32)).astype(BF16)
        tri_dot = lambda x: jnp.dot(ltri, x, preferred_element_type=F32)
        cum = tri_dot(l1) + (tri_dot(l2) + tri_dot(l3))
        g_in = jnp.exp(cum)
        g_inv = jnp.exp(-cum)
        am.append(kk_ref[0, sl, :] * jnp.exp(cum - lw))
        bm.append(b_ref[0, sl, :] * g_inv)
        km.append(k_ref[0, sl, :] * g_inv)
        rm.append(r_ref[0, sl, :] * g_in)
        vv.append(v_ref[0, sl, :])
        glast.append(g_in[c - 1:c, :])

    lak, mrb, mrk, pw, tt = [], [], [], [], []
    for ch in range(nchunk):
        for h in range(2):
            msk = head0 if h == 0 else jnp.logical_not(head0)
            a_h = jnp.where(msk, am[ch], 0.0)
            r_h = jnp.where(msk, rm[ch], 0.0)
            lab = jnp.where(strict, _bdot(a_h, bm[ch], _NT), 0.0)
            lak.append(jnp.where(strict, _bdot(a_h, km[ch], _NT), 0.0))
            mrb.append(jnp.where(incl, _bdot(r_h, bm[ch], _NT), 0.0))
            mrk.append(jnp.where(incl, _bdot(r_h, km[ch], _NT), 0.0))
            pw.append(-lab)
            tt.append(eye - lab)
    for _ in range(max(1, int(math.ceil(math.log2(c))) - 1)):
        pw = [_bdot(p_, p_, _NN) for p_ in pw]
        tt = [t_ + _bdot(t_, p_, _NN) for t_, p_ in zip(tt, pw)]

    rp, y0, pmat, qmat = [], [], [], []
    for ch in range(nchunk):
        t0, t1 = tt[2 * ch], tt[2 * ch + 1]
        ta = sel(_bdot(t0, am[ch], _NN), _bdot(t1, am[ch], _NN))
        w = sel(_bdot(t0, _bdot(lak[2 * ch], vv[ch], _NN), _NN),
                _bdot(t1, _bdot(lak[2 * ch + 1], vv[ch], _NN), _NN))
        rp.append(rm[ch] - sel(_bdot(mrb[2 * ch], ta, _NN), _bdot(mrb[2 * ch + 1], ta, _NN)))
        y0.append(sel(_bdot(mrk[2 * ch], vv[ch], _NN) - _bdot(mrb[2 * ch], w, _NN),
                      _bdot(mrk[2 * ch + 1], vv[ch], _NN) - _bdot(mrb[2 * ch + 1], w, _NN)))
        pmat.append(jnp.where(same_head, -_bdot(ta, bm[ch], _TN), 0.0))
        qmat.append(jnp.where(same_head, _bdot(vv[ch], km[ch], _TN) - _bdot(w, bm[ch], _TN), 0.0))

    st = s_ref[...]
    ys = []
    for ch in range(nchunk):
        ys.append(_bdot(rp[ch], st, _NT) + y0[ch])
        st = (st + _bdot(st, pmat[ch], _NN) + qmat[ch]) * glast[ch]
    s_ref[...] = st

    y = jnp.concatenate(ys, axis=0)
    mu = _bdot(y, seg_mean, _NN)
    yc = y - mu
    var = _bdot(yc * yc, seg_mean, _NN)
    yn = yc * lax.rsqrt(var + GN_EPS) * lng_ref[...] + lnb_ref[...]
    y_ref[0] = ((yn + bonus_ref[0]) * g_ref[0]).astype(y_ref.dtype)


def _rwkv_scan(r, lw, k, v, kk, bb, bonus, g, ln_g, ln_b):
    b, s, width = r.shape
    hp = width // LANES
    sb = min(SCAN_SB, s)
    blk = pl.BlockSpec((1, sb, LANES), lambda i, j, t: (i, t, j))
    vec = pl.BlockSpec((1, LANES), lambda i, j, t: (0, j))
    return pl.pallas_call(
        _scan_body,
        grid=(b, hp, s // sb),
        in_specs=[blk] * 8 + [vec, vec],
        out_specs=blk,
        out_shape=jax.ShapeDtypeStruct((b, s, width), BF16),
        scratch_shapes=[pltpu.VMEM((LANES, LANES), F32)],
        compiler_params=_params("parallel", "parallel", "arbitrary"),
        name="rwkv_scan",
    )(r, lw, k, v, kk, bb, bonus, g, ln_g, ln_b)


def _layer_norm(y, g, b):
    mu = jnp.mean(y, axis=-1, keepdims=True)
    yc = y - mu
    var = jnp.mean(yc * yc, axis=-1, keepdims=True)
    return yc * lax.rsqrt(var + LN_EPS) * g + b


def _merge_body(alpha, x_ref, yf_ref, yr_ref, gf_ref, gr_ref, pf_ref, pr_ref, wo_ref, g_ref, b_ref, o_ref):
    mf = jnp.dot(yf_ref[...], pf_ref[...], preferred_element_type=F32)
    mr = jnp.dot(yr_ref[...], pr_ref[...], preferred_element_type=F32)
    merged = _sigmoid(gf_ref[...]) * mf + _sigmoid(gr_ref[...]) * mr
    out = jnp.dot(merged.astype(BF16), wo_ref[...], preferred_element_type=F32)
    o_ref[...] = _layer_norm(alpha * x_ref[...] + out, g_ref[...], b_ref[...])


def _merge(alpha, x2, y_fox, y_rwkv, gates, p_fox, p_rwkv, w_o, ln_g, ln_b):
    n, d = x2.shape
    wf = y_fox.shape[1]
    wr = y_rwkv.shape[1]
    tm = min(MERGE_TM, n)
    row = lambda w, c=0: pl.BlockSpec((tm, w), lambda i: (i, c))
    full = lambda a: pl.BlockSpec(a.shape, lambda i: (0, 0))
    return pl.pallas_call(
        functools.partial(_merge_body, alpha),
        grid=(n // tm,),
        in_specs=[row(d), row(wf), row(wr), row(d, 0), row(d, 1),
                  full(p_fox), full(p_rwkv), full(w_o), full(ln_g), full(ln_b)],
        out_specs=row(d),
        out_shape=jax.ShapeDtypeStruct((n, d), F32),
        compiler_params=_params("parallel"),
        name="merge_ln",
    )(x2, y_fox, y_rwkv, gates, gates, p_fox, p_rwkv, w_o, ln_g, ln_b)


def _topk_rows(vals, k, payload=None):
    n, t = vals.shape
    rio = lax.broadcasted_iota(I32, (n, t), 0).astype(F32)
    kio = lax.broadcasted_iota(I32, (k, t), 0)
    out_v = jnp.zeros((k, t), F32)
    out_i = jnp.zeros((k, t), I32)
    cur = vals
    for it in range(k):
        mx = jnp.max(cur, axis=0, keepdims=True)
        sel = jnp.min(jnp.where(cur == mx, rio, float(n)), axis=0, keepdims=True)
        hit = rio == sel
        if payload is None:
            got = sel.astype(I32)
        else:
            got = jnp.sum(jnp.where(hit, payload, 0), axis=0, keepdims=True)
        out_v = jnp.where(kio == it, mx, out_v)
        out_i = jnp.where(kio == it, got, out_i)
        cur = jnp.where(hit, -jnp.inf, cur)
    return out_v, out_i


def _pair_candidates(top0, top1, combine):
    assert top0.shape[0] == 16 and top1.shape[0] == 16
    sub = lax.broadcasted_iota(I32, (8, top0.shape[1]), 0)
    lo = top1[0:8]
    lo_up = pltpu.roll(lo, 4, 0)
    pieces = [combine(top0[0:1], top1)]
    pieces += [combine(top0[a:a + 1], lo) for a in (1, 2, 3)]
    pieces += [jnp.where(sub < 4, combine(top0[a:a + 1], lo), combine(top0[a + 1:a + 2], lo_up))
               for a in (4, 6)]
    pieces.append(combine(top0[8:16], top1[0:1]))
    return jnp.concatenate(pieces, axis=0)


def _route_body(n_keys, x_ref, wq_ref, keys_ref, idx_ref, gate_ref):
    heads = keys_ref.shape[0] // 2
    half = keys_ref.shape[2]
    kt = PEER_TOPK
    q = jnp.dot(x_ref[...].astype(BF16), wq_ref[...], preferred_element_type=F32)
    idx_rows, gate_rows = [], []
    for h in range(heads):
        tops = []
        for p in range(2):
            c = 2 * h + p
            qc = q[:, c * half:(c + 1) * half].astype(BF16)
            s_t = _nt(keys_ref[c], qc)
            tops.append(_topk_rows(s_t, kt))
        (s0, i0), (s1, i1) = tops
        cand_s = _pair_candidates(s0, s1, lambda x, y: x + y)
        cand_i = _pair_candidates(i0, i1, lambda x, y: x * n_keys + y)
        best_s, best_i = _topk_rows(cand_s, kt, cand_i)
        e = jnp.exp(best_s - best_s[0:1, :])
        gate_rows.append(e / jnp.sum(e, axis=0, keepdims=True))
        idx_rows.append(best_i)
    idx_ref[...] = jnp.concatenate(idx_rows, axis=0).T
    gate_ref[...] = jnp.concatenate(gate_rows, axis=0).T


def _peer_route(x1, w_q, keys):
    n, d = x1.shape
    n_keys = keys.shape[1]
    slots = (keys.shape[0] // 2) * PEER_TOPK
    tm = min(ROUTE_TM, n)
    return pl.pallas_call(
        functools.partial(_route_body, n_keys),
        grid=(n // tm,),
        in_specs=[pl.BlockSpec((tm, d), lambda i: (i, 0)),
                  pl.BlockSpec(w_q.shape, lambda i: (0, 0)),
                  pl.BlockSpec(keys.shape, lambda i: (0, 0, 0))],
        out_specs=[pl.BlockSpec((tm, slots), lambda i: (i, 0))] * 2,
        out_shape=[jax.ShapeDtypeStruct((n, slots), I32), jax.ShapeDtypeStruct((n, slots), F32)],
        compiler_params=_params("parallel"),
        name="peer_route",
    )(x1, w_q, keys)


def _gelu(x):
    return 0.5 * x * (1.0 + lax.erf(x * (2.0 ** -0.5)))


def _gather_body(alpha, x_ref, gate_ref, idx_hbm, tab_hbm, g_ref, b_ref, o_ref,
                 idx_s, buf, sem_i, sem):
    ts, slots = gate_ref.shape
    d = x_ref.shape[1]
    nch = d // LANES
    tb = GATHER_TB
    rows = tb * slots
    nsub = ts // tb
    step = pl.program_id(0)

    cp_i = pltpu.make_async_copy(idx_hbm.at[pl.ds(step * ts, ts), :], idx_s, sem_i)
    cp_i.start()
    cp_i.wait()

    def issue(j, slot):
        def tok(t, carry):
            for k in range(slots):
                e = idx_s[j * tb + t, k]
                src = tab_hbm.at[pl.ds(pl.multiple_of(e * (2 * nch), 2 * nch), 2 * nch), :]
                dst = buf.at[slot, pl.ds((t * slots + k) * GATHER_PITCH, 2 * nch), :]
                pltpu.make_async_copy(src, dst, sem.at[slot]).start(priority=k % 2)
            return carry
        lax.fori_loop(0, tb, tok, 0)

    def wait(slot):
        pltpu.make_async_copy(tab_hbm.at[pl.ds(0, rows * 2 * nch), :],
                              buf.at[slot, pl.ds(0, rows * 2 * nch), :], sem.at[slot]).wait()

    eye = (lax.broadcasted_iota(I32, (slots, slots), 0)
           == lax.broadcasted_iota(I32, (slots, slots), 1)).astype(F32)
    rio = lax.broadcasted_iota(I32, (tb, 1), 0)

    issue(0, 0)

    def sub(j, carry):
        slot = lax.rem(j, 2)

        @pl.when(j + 1 < nsub)
        def _():
            issue(j + 1, 1 - slot)

        wait(slot)
        r0 = pl.multiple_of(j * tb, tb)
        xb = x_ref[pl.ds(r0, tb), :]
        gcol = _nt(eye, gate_ref[pl.ds(r0, tb), :], HI)
        acc = jnp.zeros(xb.shape, F32)
        for t in range(tb):
            base = t * slots * GATHER_PITCH
            chunk = lambda c: buf[slot, pl.ds(base + c, slots, stride=GATHER_PITCH), :]
            pu = chunk(0) * xb[t:t + 1, 0:LANES]
            for c in range(1, nch):
                pu = pu + chunk(c) * xb[t:t + 1, c * LANES:(c + 1) * LANES]
            h = jnp.sum(pu, axis=1, keepdims=True)
            w = _gelu(h) * gcol[:, t:t + 1]
            o = jnp.concatenate([jnp.sum(chunk(nch + c) * w, axis=0, keepdims=True)
                                 for c in range(nch)], axis=1)
            acc = jnp.where(rio == t, o, acc)
        o_ref[pl.ds(r0, tb), :] = _layer_norm(alpha * xb + acc, g_ref[...], b_ref[...])
        return carry

    lax.fori_loop(0, nsub, sub, 0)


def _peer_gather(alpha, x1, gates, idx, tab, ln_g, ln_b):
    n, d = x1.shape
    slots = gates.shape[1]
    ts = min(GATHER_TS, n)
    rows = GATHER_TB * slots
    any_spec = pl.BlockSpec(memory_space=pl.ANY)
    return pl.pallas_call(
        functools.partial(_gather_body, alpha),
        grid=(n // ts,),
        in_specs=[pl.BlockSpec((ts, d), lambda i: (i, 0)),
                  pl.BlockSpec((ts, slots), lambda i: (i, 0)),
                  any_spec, any_spec,
                  pl.BlockSpec((1, d), lambda i: (0, 0)),
                  pl.BlockSpec((1, d), lambda i: (0, 0))],
        out_specs=pl.BlockSpec((ts, d), lambda i: (i, 0)),
        out_shape=jax.ShapeDtypeStruct((n, d), F32),
        scratch_shapes=[pltpu.SMEM((ts, slots), I32),
                        pltpu.VMEM((2, rows * GATHER_PITCH, LANES), F32),
                        pltpu.SemaphoreType.DMA(()),
                        pltpu.SemaphoreType.DMA((2,))],
        compiler_params=_params("arbitrary"),
        name="peer_gather",
    )(x1, gates, idx, tab, ln_g, ln_b)


def _pad_rows(w, n):
    return jnp.pad(w, ((0, n - w.shape[0]), (0, 0)))


def kernel(x, w_in, fox_f_bias, rwkv_mu, rwkv_w0, rwkv_w2, rwkv_a0, rwkv_a2, rwkv_g2, rwkv_k_k, rwkv_k_a, rwkv_r_k, rwkv_ln_g, rwkv_ln_b, p_fox, p_rwkv, w_o, ln1_g, ln1_b, peer_w_q, peer_sub_keys, peer_u, peer_v, ln2_g, ln2_b):
    bsz, seq, d = x.shape
    depth = w_in.shape[0]
    n = bsz * seq
    fox_heads = fox_f_bias.shape[1]
    fw = p_fox.shape[1]
    rw = p_rwkv.shape[1]
    w_lora, a_lora, g_lora = rwkv_w2.shape[1], rwkv_a2.shape[1], rwkv_g2.shape[1]
    fox_cols = 3 * fw + fox_heads
    rwkv_cols = 3 * rw + w_lora + a_lora + g_lora
    alpha = (2 * depth) ** 0.25
    hp = fw // LANES

    seg = (jnp.arange(rw)[:, None] // HEAD_DIM == jnp.arange(rw)[None, :] // HEAD_DIM).astype(F32)

    xcur = x.reshape(n, d)
    for l in range(depth):
        w = w_in[l]
        w_qkv = w[:, :3 * fw].astype(BF16)
        wf_t = w[:, 3 * fw:fox_cols].T.astype(BF16)
        wr = w[:, fox_cols:fox_cols + rwkv_cols]
        zpad = lambda m, c: jnp.pad(m, ((0, 0), (0, c - m.shape[1])))
        o_w = 3 * rw
        w_rw = jnp.concatenate([wr[:, :o_w],
                                zpad(wr[:, o_w:o_w + w_lora], LANES),
                                zpad(wr[:, o_w + w_lora:o_w + w_lora + a_lora], LANES),
                                zpad(wr[:, o_w + w_lora + a_lora:], LANES)], axis=1).astype(BF16)
        mu = rwkv_mu[l]
        mu_p = jnp.concatenate([mu[:o_w],
                                jnp.pad(mu[o_w:o_w + w_lora], (0, LANES - w_lora)),
                                jnp.pad(mu[o_w + w_lora:o_w + w_lora + a_lora], (0, LANES - a_lora)),
                                jnp.pad(mu[o_w + w_lora + a_lora:], (0, LANES - g_lora))])[None, :]
        w_gate = w[:, fox_cols + rwkv_cols:].astype(BF16)

        qkv = _matmul(xcur, w_qkv, BF16, 512)
        rwp = _matmul(xcur, w_rw, F32, 640)
        gates = _matmul(xcur, w_gate, F32, 512)

        c_row = _forget_cumsum(xcur.reshape(bsz, seq, d), wf_t, fox_f_bias[l][:, None])
        c_row = c_row.reshape(bsz, hp, 2, seq)
        c_col = jnp.swapaxes(c_row, 2, 3)
        y_fox = _fox_attention(qkv.reshape(bsz, seq, 3 * fw), c_col, c_row, bsz, seq, fw)

        row = lambda a: a[l][None, :]
        r_, lw_, k_, v_, kk_, b_, bonus_, g_ = _rwkv_prep(
            rwp.reshape(bsz, seq, -1), mu_p, row(rwkv_w0), row(rwkv_a0), row(rwkv_k_k), row(rwkv_k_a),
            row(rwkv_r_k), _pad_rows(rwkv_w2[l], LANES), _pad_rows(rwkv_a2[l], LANES), rwkv_g2[l], seg)
        y_rwkv = _rwkv_scan(r_, lw_, k_, v_, kk_, b_, bonus_, g_, row(rwkv_ln_g), row(rwkv_ln_b))

        x1 = _merge(alpha, xcur, y_fox.reshape(n, fw), y_rwkv.reshape(n, rw), gates,
                    p_fox[l].astype(BF16), p_rwkv[l].astype(BF16), w_o[l].astype(BF16),
                    row(ln1_g), row(ln1_b))

        keys = peer_sub_keys[l]
        n_keys, half = keys.shape[2], keys.shape[3]
        idx, pg = _peer_route(x1, peer_w_q[l].astype(BF16),
                              keys.reshape(-1, n_keys, half).astype(BF16))
        nch = d // LANES
        tab = jnp.concatenate([peer_u[l].reshape(-1, nch, LANES), peer_v[l].reshape(-1, nch, LANES)],
                              axis=1).reshape(-1, LANES)
        xcur = _peer_gather(alpha, x1, pg, idx, tab, row(ln2_g), row(ln2_b))
    return xcur.reshape(bsz, seq, d)
```

```python
import functools
import math

import jax
import jax.numpy as jnp
from jax import lax
from jax.experimental import pallas as pl
from jax.experimental.pallas import tpu as pltpu

F32 = jnp.float32
BF16 = jnp.bfloat16
I32 = jnp.int32
HI = lax.Precision.HIGHEST

LANES = 128
HEAD_DIM = 64
PEER_TOPK = 16
LN_EPS = 1e-5
GN_EPS = 64e-5
NEG_BIG = -1e30
VMEM_LIMIT = 56 * 1024 * 1024

MM_TM = 256
FG_L = 512
ATT_BQ = 1024
SCAN_SB = 512
SCAN_C = 64
MERGE_TM = 512
ROUTE_TM = 256
GATHER_TS = 128
GATHER_TB = 8
GATHER_AHEAD = 2
GATHER_PITCH = 20


def _nt(a, b, precision=None):
    return lax.dot_general(a, b, (((1,), (1,)), ((), ())), precision=precision,
                           preferred_element_type=F32)


_NN = ((1,), (0,))
_NT = ((1,), (1,))
_TN = ((0,), (0,))


def _bdot(a, b, dims):
    return lax.dot_general(a.astype(BF16), b.astype(BF16), (dims, ((), ())), preferred_element_type=F32)


def _sigmoid(x):
    return 1.0 / (1.0 + jnp.exp(-x))


def _params(*sem):
    return pltpu.CompilerParams(dimension_semantics=sem, vmem_limit_bytes=VMEM_LIMIT)


def _proj_body(x_ref, wq_ref, wr_ref, wg_ref, q_o, r_o, g_o):
    xb = x_ref[...].astype(BF16)
    q_o[...] = jnp.dot(xb, wq_ref[...], preferred_element_type=F32).astype(q_o.dtype)
    r_o[...] = jnp.dot(xb, wr_ref[...], preferred_element_type=F32)
    g_o[...] = jnp.dot(xb, wg_ref[...], preferred_element_type=F32)


def _project(x, w_qkv, w_rw, w_gate):
    m, k = x.shape
    tm = min(MM_TM, m)
    full = lambda w: pl.BlockSpec(w.shape, lambda i: (0, 0))
    out = lambda w: pl.BlockSpec((tm, w.shape[1]), lambda i: (i, 0))
    return pl.pallas_call(
        _proj_body,
        grid=(m // tm,),
        in_specs=[pl.BlockSpec((tm, k), lambda i: (i, 0)), full(w_qkv), full(w_rw), full(w_gate)],
        out_specs=[out(w_qkv), out(w_rw), out(w_gate)],
        out_shape=[jax.ShapeDtypeStruct((m, w_qkv.shape[1]), BF16),
                   jax.ShapeDtypeStruct((m, w_rw.shape[1]), F32),
                   jax.ShapeDtypeStruct((m, w_gate.shape[1]), F32)],
        compiler_params=_params("parallel"),
        name="in_proj",
    )(x, w_qkv, w_rw, w_gate)


def _fgate_body(x_ref, wf_ref, b_ref, c_ref, carry_ref):
    @pl.when(pl.program_id(1) == 0)
    def _():
        carry_ref[...] = jnp.zeros_like(carry_ref)

    xb = x_ref[0].astype(BF16)
    f = _nt(wf_ref[...], xb) + b_ref[...]
    logf = -(jnp.maximum(-f, 0.0) + jnp.log1p(jnp.exp(-jnp.abs(f))))
    n = f.shape[1]
    tri = (lax.broadcasted_iota(I32, (n, n), 0) <= lax.broadcasted_iota(I32, (n, n), 1)).astype(F32)
    cs = jnp.dot(logf, tri, precision=HI, preferred_element_type=F32) + carry_ref[:, 0:1]
    c_ref[0] = cs
    carry_ref[...] = jnp.broadcast_to(cs[:, n - 1:n], carry_ref.shape)


def _forget_cumsum(x3, wf_t, bias):
    b, s, d = x3.shape
    h = wf_t.shape[0]
    blk = min(FG_L, s)
    return pl.pallas_call(
        _fgate_body,
        grid=(b, s // blk),
        in_specs=[pl.BlockSpec((1, blk, d), lambda i, j: (i, j, 0)),
                  pl.BlockSpec((h, d), lambda i, j: (0, 0)),
                  pl.BlockSpec((h, 1), lambda i, j: (0, 0))],
        out_specs=pl.BlockSpec((1, h, blk), lambda i, j: (i, 0, j)),
        out_shape=jax.ShapeDtypeStruct((b, h, s), F32),
        scratch_shapes=[pltpu.VMEM((h, LANES), F32)],
        compiler_params=_params("parallel", "arbitrary"),
        name="forget_cumsum",
    )(x3, wf_t, bias)


def _fox_body(qi_tab, ki_tab, q_ref, k_ref, v_ref, cq_ref, ck_ref, o_ref,
              qh_s, m_s, l_s, acc_s):
    p = pl.program_id(2)
    qi = qi_tab[p]
    ki = ki_tab[p]
    bq = q_ref.shape[1]
    bk = k_ref.shape[1]
    lane = lax.broadcasted_iota(I32, (1, LANES), 1)

    @pl.when(ki == 0)
    def _():
        q = q_ref[0] * jnp.asarray(HEAD_DIM ** -0.5, BF16)
        zero = jnp.zeros_like(q)
        qh_s[0] = jnp.where(lane < HEAD_DIM, q, zero)
        qh_s[1] = jnp.where(lane >= HEAD_DIM, q, zero)
        m_s[...] = jnp.full_like(m_s, NEG_BIG)
        l_s[...] = jnp.zeros_like(l_s)
        acc_s[...] = jnp.zeros_like(acc_s)

    def step(masked):
        k = k_ref[0]
        v = v_ref[0]
        for h in range(2):
            cq = cq_ref[0, 0][:, h:h + 1]
            z = _nt(qh_s[h], k) - ck_ref[0, 0][h:h + 1, :]
            if masked:
                rows = lax.broadcasted_iota(I32, (bq, bk), 0)
                cols = lax.broadcasted_iota(I32, (bq, bk), 1)
                z = jnp.where(rows >= cols, z, NEG_BIG)
            m_prev = m_s[h]
            m_new = jnp.maximum(m_prev, jnp.max(z, axis=1, keepdims=True) + cq)
            alpha = jnp.exp(m_prev - m_new)
            pr = jnp.exp(z - (m_new - cq))
            l_s[h] = alpha * l_s[h] + jnp.sum(pr, axis=1, keepdims=True)
            acc_s[h] = alpha * acc_s[h] + jnp.dot(pr.astype(BF16), v, preferred_element_type=F32)
            m_s[h] = m_new

    @pl.when(ki < qi)
    def _():
        step(False)

    @pl.when(ki == qi)
    def _():
        step(True)
        o0 = acc_s[0] / l_s[0]
        o1 = acc_s[1] / l_s[1]
        o_ref[0] = jnp.where(lane < HEAD_DIM, o0, o1).astype(o_ref.dtype)


def _fox_attention(qkv, c_col, c_row, b, s, width):
    hp = width // LANES
    bq = min(ATT_BQ, s)
    nq = s // bq
    pairs = [(i, j) for i in range(nq) for j in range(i + 1)]
    qi_tab = jnp.asarray([pq for pq, _ in pairs], I32)
    ki_tab = jnp.asarray([pk for _, pk in pairs], I32)
    grid_spec = pltpu.PrefetchScalarGridSpec(
        num_scalar_prefetch=2,
        grid=(b, hp, len(pairs)),
        in_specs=[
            pl.BlockSpec((1, bq, LANES), lambda i, j, p, qt, kt: (i, qt[p], j)),
            pl.BlockSpec((1, bq, LANES), lambda i, j, p, qt, kt: (i, kt[p], hp + j)),
            pl.BlockSpec((1, bq, LANES), lambda i, j, p, qt, kt: (i, kt[p], 2 * hp + j)),
            pl.BlockSpec((1, 1, bq, 2), lambda i, j, p, qt, kt: (i, j, qt[p], 0)),
            pl.BlockSpec((1, 1, 2, bq), lambda i, j, p, qt, kt: (i, j, 0, kt[p])),
        ],
        out_specs=pl.BlockSpec((1, bq, LANES), lambda i, j, p, qt, kt: (i, qt[p], j)),
        scratch_shapes=[pltpu.VMEM((2, bq, LANES), BF16),
                        pltpu.VMEM((2, bq, 1), F32),
                        pltpu.VMEM((2, bq, 1), F32),
                        pltpu.VMEM((2, bq, LANES), F32)],
    )
    return pl.pallas_call(
        _fox_body,
        grid_spec=grid_spec,
        out_shape=jax.ShapeDtypeStruct((b, s, width), BF16),
        compiler_params=_params("parallel", "parallel", "arbitrary"),
        name="fox_attention",
    )(qi_tab, ki_tab, qkv, qkv, qkv, c_col, c_row)


def _rwkv_features(shifted, w0, a0, k_k, k_a, r_k, w2, a2, g2, seg):
    r, k, v, wl, al, gl = shifted
    hdot = lambda x, y: jnp.dot(x, y, precision=HI, preferred_element_type=F32)
    w_pre = w0 + hdot(jnp.tanh(wl), w2)
    lw = (-math.exp(-0.5)) * _sigmoid(w_pre)
    a = _sigmoid(a0 + hdot(al, a2))
    g = hdot(_sigmoid(gl), g2)
    kk = k * k_k
    kkn = kk / jnp.maximum(jnp.sqrt(hdot(kk * kk, seg)), 1e-12)
    kmod = k * (1.0 + (a - 1.0) * k_a)
    bonus = hdot(r * kmod * r_k, seg) * v
    return r, lw, kmod, v, kkn, kkn * a, bonus, g


def _scan_body(*refs):
    p_refs, mu_refs = refs[0:6], refs[6:12]
    w0_ref, a0_ref, kk_ref, ka_ref, rk_ref, w2_ref, a2_ref, g2_ref, lng_ref, lnb_ref = refs[12:22]
    y_ref, s_ref, carry_ref = refs[22:25]
    sb = y_ref.shape[1]
    c = min(SCAN_C, sb)

    @pl.when(pl.program_id(2) == 0)
    def _():
        s_ref[...] = jnp.zeros_like(s_ref)
        carry_ref[...] = jnp.zeros_like(carry_ref)

    lane = lax.broadcasted_iota(I32, (1, LANES), 1)
    head0 = lane < HEAD_DIM
    ti = lax.broadcasted_iota(I32, (c, c), 0)
    si = lax.broadcasted_iota(I32, (c, c), 1)
    incl = ti >= si
    strict = ti > si
    ltri = incl.astype(BF16)
    eye = (ti == si).astype(F32)
    ri = lax.broadcasted_iota(I32, (LANES, LANES), 0)
    ci = lax.broadcasted_iota(I32, (LANES, LANES), 1)
    same_head = (ri < HEAD_DIM) == (ci < HEAD_DIM)
    seg_mean = same_head.astype(F32) * (1.0 / HEAD_DIM)
    nchunk = sb // c
    sel = lambda x0, x1: jnp.where(head0, x0, x1)


    row0 = lax.broadcasted_iota(I32, (sb, LANES), 0) == 0
    shifted = []
    for i in range(6):
        p = p_refs[i][0]
        prev = jnp.where(row0, jnp.broadcast_to(carry_ref[i, 0:1, :], p.shape), pltpu.roll(p, 1, 0))
        carry_ref[i] = jnp.broadcast_to(p[sb - 1:sb, :], carry_ref.shape[1:])
        shifted.append(p + mu_refs[i][...] * (prev - p))
    r_all, lw_all, k_all, v_all, kk_all, b_all, bonus, gate = _rwkv_features(
        shifted, w0_ref[...], a0_ref[...], kk_ref[...], ka_ref[...], rk_ref[...],
        w2_ref[...], a2_ref[...], g2_ref[...], same_head.astype(F32))

    am, bm, km, rm, vv, glast = [], [], [], [], [], []
    for ch in range(nchunk):
        sl = slice(ch * c, (ch + 1) * c)
        lw = lw_all[sl]
        l1 = lw.astype(BF16)
        r1 = lw - l1.astype(F32)
        l2 = r1.astype(BF16)
        l3 = (r1 - l2.astype(F32)).astype(BF16)
        tri_dot = lambda x: jnp.dot(ltri, x, preferred_element_type=F32)
        cum = tri_dot(l1) + (tri_dot(l2) + tri_dot(l3))
        g_in = jnp.exp(cum)
        g_inv = jnp.exp(-cum)
        am.append(kk_all[sl] * jnp.exp(cum - lw))
        bm.append(b_all[sl] * g_inv)
        km.append(k_all[sl] * g_inv)
        rm.append(r_all[sl] * g_in)
        vv.append(v_all[sl])
        glast.append(g_in[c - 1:c, :])

    lak, mrb, mrk, pw, tt = [], [], [], [], []
    for ch in range(nchunk):
        for h in range(2):
            msk = head0 if h == 0 else jnp.logical_not(head0)
            a_h = jnp.where(msk, am[ch], 0.0)
            r_h = jnp.where(msk, rm[ch], 0.0)
            lab = jnp.where(strict, _bdot(a_h, bm[ch], _NT), 0.0)
            lak.append(jnp.where(strict, _bdot(a_h, km[ch], _NT), 0.0))
            mrb.append(jnp.where(incl, _bdot(r_h, bm[ch], _NT), 0.0))
            mrk.append(jnp.where(incl, _bdot(r_h, km[ch], _NT), 0.0))
            pw.append(-lab)
            tt.append(eye - lab)
    for _ in range(max(1, int(math.ceil(math.log2(c))) - 1)):
        pw = [_bdot(p_, p_, _NN) for p_ in pw]
        tt = [t_ + _bdot(t_, p_, _NN) for t_, p_ in zip(tt, pw)]

    rp, y0, pmat, qmat = [], [], [], []
    for ch in range(nchunk):
        t0, t1 = tt[2 * ch], tt[2 * ch + 1]
        ta = sel(_bdot(t0, am[ch], _NN), _bdot(t1, am[ch], _NN))
        w = sel(_bdot(t0, _bdot(lak[2 * ch], vv[ch], _NN), _NN),
                _bdot(t1, _bdot(lak[2 * ch + 1], vv[ch], _NN), _NN))
        rp.append(rm[ch] - sel(_bdot(mrb[2 * ch], ta, _NN), _bdot(mrb[2 * ch + 1], ta, _NN)))
        y0.append(sel(_bdot(mrk[2 * ch], vv[ch], _NN) - _bdot(mrb[2 * ch], w, _NN),
                      _bdot(mrk[2 * ch + 1], vv[ch], _NN) - _bdot(mrb[2 * ch + 1], w, _NN)))
        pmat.append(jnp.where(same_head, -_bdot(ta, bm[ch], _TN), 0.0))
        qmat.append(jnp.where(same_head, _bdot(vv[ch], km[ch], _TN) - _bdot(w, bm[ch], _TN), 0.0))

    st = s_ref[...]
    ys = []
    for ch in range(nchunk):
        ys.append(_bdot(rp[ch], st, _NT) + y0[ch])
        st = (st + _bdot(st, pmat[ch], _NN) + qmat[ch]) * glast[ch]
    s_ref[...] = st

    y = jnp.concatenate(ys, axis=0)
    mu = _bdot(y, seg_mean, _NN)
    yc = y - mu
    var = _bdot(yc * yc, seg_mean, _NN)
    yn = yc * lax.rsqrt(var + GN_EPS) * lng_ref[...] + lnb_ref[...]
    y_ref[0] = ((yn + bonus) * gate).astype(y_ref.dtype)


def _rwkv_scan(proj, mu, w0, a0, k_k, k_a, r_k, w2p, a2p, g2, ln_g, ln_b):
    b, s, _ = proj.shape
    width = w0.shape[1]
    hp = width // LANES
    sb = min(SCAN_SB, s)
    cols = [lambda j: j, lambda j: hp + j, lambda j: 2 * hp + j,
            lambda j: 3 * hp, lambda j: 3 * hp + 1, lambda j: 3 * hp + 2]
    p_specs = [pl.BlockSpec((1, sb, LANES), lambda i, j, t, f=f: (i, t, f(j))) for f in cols]
    mu_specs = [pl.BlockSpec((1, LANES), lambda i, j, t, f=f: (0, f(j))) for f in cols]
    vec = pl.BlockSpec((1, LANES), lambda i, j, t: (0, j))
    mat = pl.BlockSpec((LANES, LANES), lambda i, j, t: (0, j))
    return pl.pallas_call(
        _scan_body,
        grid=(b, hp, s // sb),
        in_specs=p_specs + mu_specs + [vec] * 5 + [mat] * 3 + [vec, vec],
        out_specs=pl.BlockSpec((1, sb, LANES), lambda i, j, t: (i, t, j)),
        out_shape=jax.ShapeDtypeStruct((b, s, width), BF16),
        scratch_shapes=[pltpu.VMEM((LANES, LANES), F32), pltpu.VMEM((6, 8, LANES), F32)],
        compiler_params=_params("parallel", "parallel", "arbitrary"),
        name="rwkv_scan",
    )(*([proj] * 6), *([mu] * 6), w0, a0, k_k, k_a, r_k, w2p, a2p, g2, ln_g, ln_b)


def _layer_norm(y, g, b):
    mu = jnp.mean(y, axis=-1, keepdims=True)
    yc = y - mu
    var = jnp.mean(yc * yc, axis=-1, keepdims=True)
    return yc * lax.rsqrt(var + LN_EPS) * g + b


def _merge_body(alpha, x_ref, yf_ref, yr_ref, gf_ref, gr_ref, pf_ref, pr_ref, wo_ref, g_ref, b_ref, o_ref):
    mf = jnp.dot(yf_ref[...], pf_ref[...], preferred_element_type=F32)
    mr = jnp.dot(yr_ref[...], pr_ref[...], preferred_element_type=F32)
    merged = _sigmoid(gf_ref[...]) * mf + _sigmoid(gr_ref[...]) * mr
    out = jnp.dot(merged.astype(BF16), wo_ref[...], preferred_element_type=F32)
    o_ref[...] = _layer_norm(alpha * x_ref[...] + out, g_ref[...], b_ref[...])


def _merge(alpha, x2, y_fox, y_rwkv, gates, p_fox, p_rwkv, w_o, ln_g, ln_b):
    n, d = x2.shape
    wf = y_fox.shape[1]
    wr = y_rwkv.shape[1]
    tm = min(MERGE_TM, n)
    row = lambda w, c=0: pl.BlockSpec((tm, w), lambda i: (i, c))
    full = lambda a: pl.BlockSpec(a.shape, lambda i: (0, 0))
    return pl.pallas_call(
        functools.partial(_merge_body, alpha),
        grid=(n // tm,),
        in_specs=[row(d), row(wf), row(wr), row(d, 0), row(d, 1),
                  full(p_fox), full(p_rwkv), full(w_o), full(ln_g), full(ln_b)],
        out_specs=row(d),
        out_shape=jax.ShapeDtypeStruct((n, d), F32),
        compiler_params=_params("parallel"),
        name="merge_ln",
    )(x2, y_fox, y_rwkv, gates, gates, p_fox, p_rwkv, w_o, ln_g, ln_b)


def _topk_rows(vals, k, payload=None):
    n, t = vals.shape
    rio = lax.broadcasted_iota(I32, (n, t), 0).astype(F32)
    kio = lax.broadcasted_iota(I32, (k, t), 0)
    out_v = jnp.zeros((k, t), F32)
    out_i = jnp.zeros((k, t), I32)
    cur = vals
    for it in range(k):
        mx = jnp.max(cur, axis=0, keepdims=True)
        sel = jnp.min(jnp.where(cur == mx, rio, float(n)), axis=0, keepdims=True)
        hit = rio == sel
        if payload is None:
            got = sel.astype(I32)
        else:
            got = jnp.sum(jnp.where(hit, payload, 0), axis=0, keepdims=True)
        out_v = jnp.where(kio == it, mx, out_v)
        out_i = jnp.where(kio == it, got, out_i)
        cur = jnp.where(hit, -jnp.inf, cur)
    return out_v, out_i


def _pair_candidates(top0, top1, combine):
    assert top0.shape[0] == 16 and top1.shape[0] == 16
    sub = lax.broadcasted_iota(I32, (8, top0.shape[1]), 0)
    lo = top1[0:8]
    lo_up = pltpu.roll(lo, 4, 0)
    pieces = [combine(top0[0:1], top1)]
    pieces += [combine(top0[a:a + 1], lo) for a in (1, 2, 3)]
    pieces += [jnp.where(sub < 4, combine(top0[a:a + 1], lo), combine(top0[a + 1:a + 2], lo_up))
               for a in (4, 6)]
    pieces.append(combine(top0[8:16], top1[0:1]))
    return jnp.concatenate(pieces, axis=0)


def _route_body(n_keys, x_ref, wq_ref, keys_ref, idx_ref, gate_ref):
    heads = keys_ref.shape[0] // 2
    half = keys_ref.shape[2]
    kt = PEER_TOPK
    q = jnp.dot(x_ref[...].astype(BF16), wq_ref[...], preferred_element_type=F32)
    idx_rows, gate_rows = [], []
    for h in range(heads):
        tops = []
        for p in range(2):
            c = 2 * h + p
            qc = q[:, c * half:(c + 1) * half].astype(BF16)
            s_t = _nt(keys_ref[c], qc)
            tops.append(_topk_rows(s_t, kt))
        (s0, i0), (s1, i1) = tops
        cand_s = _pair_candidates(s0, s1, lambda x, y: x + y)
        cand_i = _pair_candidates(i0, i1, lambda x, y: x * n_keys + y)
        best_s, best_i = _topk_rows(cand_s, kt, cand_i)
        e = jnp.exp(best_s - best_s[0:1, :])
        gate_rows.append(e / jnp.sum(e, axis=0, keepdims=True))
        idx_rows.append(best_i)
    idx_ref[...] = jnp.concatenate(idx_rows, axis=0).T
    gate_ref[...] = jnp.concatenate(gate_rows, axis=0).T


def _peer_route(x1, w_q, keys):
    n, d = x1.shape
    n_keys = keys.shape[1]
    slots = (keys.shape[0] // 2) * PEER_TOPK
    tm = min(ROUTE_TM, n)
    return pl.pallas_call(
        functools.partial(_route_body, n_keys),
        grid=(n // tm,),
        in_specs=[pl.BlockSpec((tm, d), lambda i: (i, 0)),
                  pl.BlockSpec(w_q.shape, lambda i: (0, 0)),
                  pl.BlockSpec(keys.shape, lambda i: (0, 0, 0))],
        out_specs=[pl.BlockSpec((tm, slots), lambda i: (i, 0))] * 2,
        out_shape=[jax.ShapeDtypeStruct((n, slots), I32), jax.ShapeDtypeStruct((n, slots), F32)],
        compiler_params=_params("parallel"),
        name="peer_route",
    )(x1, w_q, keys)


def _gelu(x):
    return 0.5 * x * (1.0 + lax.erf(x * (2.0 ** -0.5)))


def _pack_tables(u, v):
    e, d = u.shape
    return jnp.concatenate([u.reshape(e, d // LANES, LANES), v.reshape(e, d // LANES, LANES)],
                           axis=1).reshape(-1, LANES)


def _gather_body(alpha, x_ref, gate_ref, idx_hbm, tab_hbm, g_ref, b_ref, o_ref, idx_s, buf, sem_i, sem):
    ts, slots = gate_ref.shape
    d = x_ref.shape[1]
    nw = d // LANES
    erows = 2 * nw
    tb = GATHER_TB
    rows = tb * slots
    nsub = ts // tb
    step = pl.program_id(0)
    nsteps = pl.num_programs(0)
    ib = step % 2
    nbuf = GATHER_AHEAD + 1
    kgrp = slots // (2 * nw)

    def idx_copy(s, b):
        return pltpu.make_async_copy(idx_hbm.at[pl.ds(s * ts, ts), :], idx_s.at[b], sem_i.at[b])

    def issue(b, row, t, k0, slot):
        for k in range(k0, k0 + kgrp):
            e = idx_s[b, row, k]
            src = tab_hbm.at[pl.ds(pl.multiple_of(e * erows, erows), erows), :]
            dst = buf.at[slot, pl.ds((t * slots + k) * GATHER_PITCH, erows), :]
            pltpu.make_async_copy(src, dst, sem.at[slot]).start(priority=k % 2)

    def wait(slot):
        pltpu.make_async_copy(tab_hbm.at[pl.ds(0, rows * erows), :],
                              buf.at[slot, pl.ds(0, rows * erows), :], sem.at[slot]).wait()

    @pl.when(step == 0)
    def _():
        idx_copy(0, 0).start()
        idx_copy(0, 0).wait()
        for j in range(GATHER_AHEAD):
            for t in range(tb):
                for k0 in range(0, slots, kgrp):
                    issue(0, j * tb + t, t, k0, j)

    @pl.when(step + 1 < nsteps)
    def _():
        idx_copy(step + 1, 1 - ib).start()

    eye = (lax.broadcasted_iota(I32, (slots, slots), 0)
           == lax.broadcasted_iota(I32, (slots, slots), 1)).astype(F32)
    rio = lax.broadcasted_iota(I32, (tb, 1), 0)
    tio = lax.broadcasted_iota(I32, (slots, tb), 1)

    def compute(j, slot, nb, nj, nslot):
        r0 = pl.multiple_of(j * tb, tb)
        xb = x_ref[pl.ds(r0, tb), :]
        gcol = _nt(eye, gate_ref[pl.ds(r0, tb), :], HI)
        chunk = lambda t, c: buf[slot, pl.ds(t * slots * GATHER_PITCH + c, slots, stride=GATHER_PITCH), :]
        hmat = jnp.zeros((slots, tb), F32)
        for t in range(tb):
            pu = None
            for c in range(nw):
                issue(nb, nj * tb + t, t, c * kgrp, nslot)
                term = chunk(t, c) * xb[t:t + 1, c * LANES:(c + 1) * LANES]
                pu = term if pu is None else pu + term
            hmat = jnp.where(tio == t, jnp.sum(pu, axis=1, keepdims=True), hmat)
        wmat = _gelu(hmat) * gcol
        acc = jnp.zeros(xb.shape, F32)
        for t in range(tb):
            w = wmat[:, t:t + 1]
            outs = []
            for c in range(nw):
                issue(nb, nj * tb + t, t, (nw + c) * kgrp, nslot)
                outs.append(jnp.sum(chunk(t, nw + c) * w, axis=0, keepdims=True))
            acc = jnp.where(rio == t, jnp.concatenate(outs, axis=1), acc)
        o_ref[pl.ds(r0, tb), :] = _layer_norm(alpha * xb + acc, g_ref[...], b_ref[...])

    more = step + 1 < nsteps
    g0 = step * nsub

    def sub(j, carry):
        @pl.when(jnp.logical_and(j == nsub - GATHER_AHEAD, more))
        def _():
            idx_copy(step + 1, 1 - ib).wait()

        slot = (g0 + j) % nbuf
        wait(slot)
        jn = j + GATHER_AHEAD
        over = jn >= nsub
        compute(j, slot, jnp.where(jnp.logical_and(over, more), 1 - ib, ib), jnp.where(over, jn - nsub, jn),
                (g0 + jn) % nbuf)
        return carry

    lax.fori_loop(0, nsub, sub, 0)

    @pl.when(jnp.logical_not(more))
    def _():
        for a in range(GATHER_AHEAD):
            wait((g0 + nsub + a) % nbuf)


def _peer_gather(alpha, x1, gates, idx, tab, ln_g, ln_b):
    n, d = x1.shape
    slots = gates.shape[1]
    ts = min(GATHER_TS, n)
    rows = GATHER_TB * slots
    any_spec = pl.BlockSpec(memory_space=pl.ANY)
    return pl.pallas_call(
        functools.partial(_gather_body, alpha),
        grid=(n // ts,),
        in_specs=[pl.BlockSpec((ts, d), lambda i: (i, 0)),
                  pl.BlockSpec((ts, slots), lambda i: (i, 0)),
                  any_spec, any_spec,
                  pl.BlockSpec((1, d), lambda i: (0, 0)),
                  pl.BlockSpec((1, d), lambda i: (0, 0))],
        out_specs=pl.BlockSpec((ts, d), lambda i: (i, 0)),
        out_shape=jax.ShapeDtypeStruct((n, d), F32),
        scratch_shapes=[pltpu.SMEM((2, ts, slots), I32),
                        pltpu.VMEM((GATHER_AHEAD + 1, rows * GATHER_PITCH, LANES), F32),
                        pltpu.SemaphoreType.DMA((2,)),
                        pltpu.SemaphoreType.DMA((GATHER_AHEAD + 1,))],
        compiler_params=_params("arbitrary"),
        name="peer_gather",
    )(x1, gates, idx, tab, ln_g, ln_b)


def _pad_rows(w, n):
    return jnp.pad(w, ((0, n - w.shape[0]), (0, 0)))


def kernel(x, w_in, fox_f_bias, rwkv_mu, rwkv_w0, rwkv_w2, rwkv_a0, rwkv_a2, rwkv_g2, rwkv_k_k, rwkv_k_a, rwkv_r_k, rwkv_ln_g, rwkv_ln_b, p_fox, p_rwkv, w_o, ln1_g, ln1_b, peer_w_q, peer_sub_keys, peer_u, peer_v, ln2_g, ln2_b):
    bsz, seq, d = x.shape
    depth = w_in.shape[0]
    n = bsz * seq
    fox_heads = fox_f_bias.shape[1]
    fw = p_fox.shape[1]
    rw = p_rwkv.shape[1]
    w_lora, a_lora, g_lora = rwkv_w2.shape[1], rwkv_a2.shape[1], rwkv_g2.shape[1]
    fox_cols = 3 * fw + fox_heads
    rwkv_cols = 3 * rw + w_lora + a_lora + g_lora
    alpha = (2 * depth) ** 0.25
    hp = fw // LANES

    xcur = x.reshape(n, d)
    for l in range(depth):
        w = w_in[l]
        w_qkv = w[:, :3 * fw].astype(BF16)
        wf_t = w[:, 3 * fw:fox_cols].T.astype(BF16)
        wr = w[:, fox_cols:fox_cols + rwkv_cols]
        zpad = lambda m, c: jnp.pad(m, ((0, 0), (0, c - m.shape[1])))
        o_w = 3 * rw
        w_rw = jnp.concatenate([wr[:, :o_w],
                                zpad(wr[:, o_w:o_w + w_lora], LANES),
                                zpad(wr[:, o_w + w_lora:o_w + w_lora + a_lora], LANES),
                                zpad(wr[:, o_w + w_lora + a_lora:], LANES)], axis=1).astype(BF16)
        mu = rwkv_mu[l]
        mu_p = jnp.concatenate([mu[:o_w],
                                jnp.pad(mu[o_w:o_w + w_lora], (0, LANES - w_lora)),
                                jnp.pad(mu[o_w + w_lora:o_w + w_lora + a_lora], (0, LANES - a_lora)),
                                jnp.pad(mu[o_w + w_lora + a_lora:], (0, LANES - g_lora))])[None, :]
        w_gate = w[:, fox_cols + rwkv_cols:].astype(BF16)

        qkv, rwp, gates = _project(xcur, w_qkv, w_rw, w_gate)

        c_row = _forget_cumsum(xcur.reshape(bsz, seq, d), wf_t, fox_f_bias[l][:, None])
        c_row = c_row.reshape(bsz, hp, 2, seq)
        c_col = jnp.swapaxes(c_row, 2, 3)
        y_fox = _fox_attention(qkv.reshape(bsz, seq, 3 * fw), c_col, c_row, bsz, seq, fw)

        row = lambda a: a[l][None, :]
        y_rwkv = _rwkv_scan(rwp.reshape(bsz, seq, -1), mu_p, row(rwkv_w0), row(rwkv_a0), row(rwkv_k_k),
                            row(rwkv_k_a), row(rwkv_r_k), _pad_rows(rwkv_w2[l], LANES),
                            _pad_rows(rwkv_a2[l], LANES), rwkv_g2[l], row(rwkv_ln_g), row(rwkv_ln_b))

        x1 = _merge(alpha, xcur, y_fox.reshape(n, fw), y_rwkv.reshape(n, rw), gates,
                    p_fox[l].astype(BF16), p_rwkv[l].astype(BF16), w_o[l].astype(BF16),
                    row(ln1_g), row(ln1_b))

        keys = peer_sub_keys[l]
        n_keys, half = keys.shape[2], keys.shape[3]
        idx, pg = _peer_route(x1, peer_w_q[l].astype(BF16),
                              keys.reshape(-1, n_keys, half).astype(BF16))
        xcur = _peer_gather(alpha, x1, pg, idx, _pack_tables(peer_u[l], peer_v[l]), row(ln2_g), row(ln2_b))
    return xcur.reshape(bsz, seq, d)
```

```python
import functools
import math

import jax
import jax.numpy as jnp
from jax import lax
from jax.experimental import pallas as pl
from jax.experimental.pallas import tpu as pltpu

F32 = jnp.float32
BF16 = jnp.bfloat16
I32 = jnp.int32
HI = lax.Precision.HIGHEST

LANES = 128
HEAD_DIM = 64
PEER_TOPK = 16
LN_EPS = 1e-5
GN_EPS = 64e-5
NEG_BIG = -1e30
VMEM_LIMIT = 56 * 1024 * 1024

MM_TM = 256
FG_L = 512
ATT_BQ = 1024
ATT_RB = 256
SCAN_SB = 512
SCAN_C = 64
MERGE_TM = 512
ROUTE_TM = 256
GATHER_TS = 128
GATHER_TB = 8
GATHER_AHEAD = 2
GATHER_PITCH = 20


def _nt(a, b, precision=None):
    return lax.dot_general(a, b, (((1,), (1,)), ((), ())), precision=precision,
                           preferred_element_type=F32)


_NN = ((1,), (0,))
_NT = ((1,), (1,))
_TN = ((0,), (0,))


def _bdot(a, b, dims):
    return lax.dot_general(a.astype(BF16), b.astype(BF16), (dims, ((), ())), preferred_element_type=F32)


def _sigmoid(x):
    return 1.0 / (1.0 + jnp.exp(-x))


def _params(*sem):
    return pltpu.CompilerParams(dimension_semantics=sem, vmem_limit_bytes=VMEM_LIMIT)


def _proj_body(x_ref, wq_ref, wr_ref, wg_ref, q_o, r_o, g_o):
    xb = x_ref[...].astype(BF16)
    q_o[...] = jnp.dot(xb, wq_ref[...], preferred_element_type=F32).astype(q_o.dtype)
    r_o[...] = jnp.dot(xb, wr_ref[...], preferred_element_type=F32)
    g_o[...] = jnp.dot(xb, wg_ref[...], preferred_element_type=F32)


def _project(x, w_qkv, w_rw, w_gate):
    m, k = x.shape
    tm = min(MM_TM, m)
    full = lambda w: pl.BlockSpec(w.shape, lambda i: (0, 0))
    out = lambda w: pl.BlockSpec((tm, w.shape[1]), lambda i: (i, 0))
    return pl.pallas_call(
        _proj_body,
        grid=(m // tm,),
        in_specs=[pl.BlockSpec((tm, k), lambda i: (i, 0)), full(w_qkv), full(w_rw), full(w_gate)],
        out_specs=[out(w_qkv), out(w_rw), out(w_gate)],
        out_shape=[jax.ShapeDtypeStruct((m, w_qkv.shape[1]), BF16),
                   jax.ShapeDtypeStruct((m, w_rw.shape[1]), F32),
                   jax.ShapeDtypeStruct((m, w_gate.shape[1]), F32)],
        compiler_params=_params("parallel"),
        name="in_proj",
    )(x, w_qkv, w_rw, w_gate)


def _fgate_body(x_ref, wf_ref, b_ref, c_ref, carry_ref):
    @pl.when(pl.program_id(1) == 0)
    def _():
        carry_ref[...] = jnp.zeros_like(carry_ref)

    xb = x_ref[0].astype(BF16)
    f = _nt(wf_ref[...], xb) + b_ref[...]
    logf = -(jnp.maximum(-f, 0.0) + jnp.log1p(jnp.exp(-jnp.abs(f))))
    n = f.shape[1]
    tri = (lax.broadcasted_iota(I32, (n, n), 0) <= lax.broadcasted_iota(I32, (n, n), 1)).astype(F32)
    cs = jnp.dot(logf, tri, precision=HI, preferred_element_type=F32) + carry_ref[:, 0:1]
    c_ref[0] = cs
    carry_ref[...] = jnp.broadcast_to(cs[:, n - 1:n], carry_ref.shape)


def _forget_cumsum(x3, wf_t, bias):
    b, s, d = x3.shape
    h = wf_t.shape[0]
    blk = min(FG_L, s)
    return pl.pallas_call(
        _fgate_body,
        grid=(b, s // blk),
        in_specs=[pl.BlockSpec((1, blk, d), lambda i, j: (i, j, 0)),
                  pl.BlockSpec((h, d), lambda i, j: (0, 0)),
                  pl.BlockSpec((h, 1), lambda i, j: (0, 0))],
        out_specs=pl.BlockSpec((1, h, blk), lambda i, j: (i, 0, j)),
        out_shape=jax.ShapeDtypeStruct((b, h, s), F32),
        scratch_shapes=[pltpu.VMEM((h, LANES), F32)],
        compiler_params=_params("parallel", "arbitrary"),
        name="forget_cumsum",
    )(x3, wf_t, bias)


def _fox_body(qi_tab, ki_tab, q_ref, k_ref, v_ref, cq_ref, ck_ref, o_ref,
              qh_s, m_s, l_s, acc_s):
    p = pl.program_id(2)
    qi = qi_tab[p]
    ki = ki_tab[p]
    bq = q_ref.shape[1]
    bk = k_ref.shape[1]
    lane = lax.broadcasted_iota(I32, (1, LANES), 1)

    @pl.when(ki == 0)
    def _():
        q = q_ref[0] * jnp.asarray(HEAD_DIM ** -0.5, BF16)
        zero = jnp.zeros_like(q)
        qh_s[0] = jnp.where(lane < HEAD_DIM, q, zero)
        qh_s[1] = jnp.where(lane >= HEAD_DIM, q, zero)
        m_s[...] = jnp.full_like(m_s, NEG_BIG)
        l_s[...] = jnp.zeros_like(l_s)
        acc_s[...] = jnp.zeros_like(acc_s)

    def step(masked):
        rb = min(ATT_RB, bq)
        units = [(r, h) for r in range(bq // rb) for h in range(2)]

        def scores(unit):
            r, h = unit
            z = _nt(qh_s[h, r * rb:(r + 1) * rb, :], k_ref[0]) - ck_ref[0, 0][h:h + 1, :]
            if masked:
                rows = lax.broadcasted_iota(I32, (rb, bk), 0) + r * rb
                cols = lax.broadcasted_iota(I32, (rb, bk), 1)
                z = jnp.where(rows >= cols, z, NEG_BIG)
            return z

        def finish(unit, z):
            r, h = unit
            rs = slice(r * rb, (r + 1) * rb)
            cq = cq_ref[0, 0, rs, :][:, h:h + 1]
            m_prev = m_s[h, rs, :]
            m_new = jnp.maximum(m_prev, jnp.max(z, axis=1, keepdims=True) + cq)
            alpha = jnp.exp(m_prev - m_new)
            pr = jnp.exp(z - (m_new - cq))
            l_s[h, rs, :] = alpha * l_s[h, rs, :] + jnp.sum(pr, axis=1, keepdims=True)
            acc_s[h, rs, :] = alpha * acc_s[h, rs, :] + jnp.dot(
                pr.astype(BF16), v_ref[0], preferred_element_type=F32)
            m_s[h, rs, :] = m_new

        z = scores(units[0])
        for i, unit in enumerate(units):
            z_next = scores(units[i + 1]) if i + 1 < len(units) else None
            finish(unit, z)
            z = z_next

    @pl.when(ki < qi)
    def _():
        step(False)

    @pl.when(ki == qi)
    def _():
        step(True)
        o0 = acc_s[0] / l_s[0]
        o1 = acc_s[1] / l_s[1]
        o_ref[0] = jnp.where(lane < HEAD_DIM, o0, o1).astype(o_ref.dtype)


def _fox_attention(qkv, c_col, c_row, b, s, width):
    hp = width // LANES
    bq = min(ATT_BQ, s)
    nq = s // bq
    pairs = [(i, j) for i in range(nq) for j in range(i + 1)]
    qi_tab = jnp.asarray([pq for pq, _ in pairs], I32)
    ki_tab = jnp.asarray([pk for _, pk in pairs], I32)
    grid_spec = pltpu.PrefetchScalarGridSpec(
        num_scalar_prefetch=2,
        grid=(b, hp, len(pairs)),
        in_specs=[
            pl.BlockSpec((1, bq, LANES), lambda i, j, p, qt, kt: (i, qt[p], j)),
            pl.BlockSpec((1, bq, LANES), lambda i, j, p, qt, kt: (i, kt[p], hp + j)),
            pl.BlockSpec((1, bq, LANES), lambda i, j, p, qt, kt: (i, kt[p], 2 * hp + j)),
            pl.BlockSpec((1, 1, bq, 2), lambda i, j, p, qt, kt: (i, j, qt[p], 0)),
            pl.BlockSpec((1, 1, 2, bq), lambda i, j, p, qt, kt: (i, j, 0, kt[p])),
        ],
        out_specs=pl.BlockSpec((1, bq, LANES), lambda i, j, p, qt, kt: (i, qt[p], j)),
        scratch_shapes=[pltpu.VMEM((2, bq, LANES), BF16),
                        pltpu.VMEM((2, bq, 1), F32),
                        pltpu.VMEM((2, bq, 1), F32),
                        pltpu.VMEM((2, bq, LANES), F32)],
    )
    return pl.pallas_call(
        _fox_body,
        grid_spec=grid_spec,
        out_shape=jax.ShapeDtypeStruct((b, s, width), BF16),
        compiler_params=_params("parallel", "parallel", "arbitrary"),
        name="fox_attention",
    )(qi_tab, ki_tab, qkv, qkv, qkv, c_col, c_row)


def _rwkv_features(shifted, w0, a0, k_k, k_a, r_k, w2, a2, g2, seg):
    r, k, v, wl, al, gl = shifted
    hdot = lambda x, y: jnp.dot(x, y, precision=HI, preferred_element_type=F32)
    w_pre = w0 + hdot(jnp.tanh(wl), w2)
    lw = (-math.exp(-0.5)) * _sigmoid(w_pre)
    a = _sigmoid(a0 + hdot(al, a2))
    g = hdot(_sigmoid(gl), g2)
    kk = k * k_k
    kkn = kk / jnp.maximum(jnp.sqrt(hdot(kk * kk, seg)), 1e-12)
    kmod = k * (1.0 + (a - 1.0) * k_a)
    bonus = hdot(r * kmod * r_k, seg) * v
    return r, lw, kmod, v, kkn, kkn * a, bonus, g


def _scan_body(*refs):
    p_refs, mu_refs = refs[0:6], refs[6:12]
    w0_ref, a0_ref, kk_ref, ka_ref, rk_ref, w2_ref, a2_ref, g2_ref, lng_ref, lnb_ref = refs[12:22]
    y_ref, s_ref, carry_ref = refs[22:25]
    sb = y_ref.shape[1]
    c = min(SCAN_C, sb)

    @pl.when(pl.program_id(2) == 0)
    def _():
        s_ref[...] = jnp.zeros_like(s_ref)
        carry_ref[...] = jnp.zeros_like(carry_ref)

    lane = lax.broadcasted_iota(I32, (1, LANES), 1)
    head0 = lane < HEAD_DIM
    ti = lax.broadcasted_iota(I32, (c, c), 0)
    si = lax.broadcasted_iota(I32, (c, c), 1)
    incl = ti >= si
    strict = ti > si
    ltri = incl.astype(BF16)
    eye = (ti == si).astype(F32)
    ri = lax.broadcasted_iota(I32, (LANES, LANES), 0)
    ci = lax.broadcasted_iota(I32, (LANES, LANES), 1)
    same_head = (ri < HEAD_DIM) == (ci < HEAD_DIM)
    seg_mean = same_head.astype(F32) * (1.0 / HEAD_DIM)
    nchunk = sb // c
    sel = lambda x0, x1: jnp.where(head0, x0, x1)


    row0 = lax.broadcasted_iota(I32, (sb, LANES), 0) == 0
    shifted = []
    for i in range(6):
        p = p_refs[i][0]
        prev = jnp.where(row0, jnp.broadcast_to(carry_ref[i, 0:1, :], p.shape), pltpu.roll(p, 1, 0))
        carry_ref[i] = jnp.broadcast_to(p[sb - 1:sb, :], carry_ref.shape[1:])
        shifted.append(p + mu_refs[i][...] * (prev - p))
    r_all, lw_all, k_all, v_all, kk_all, b_all, bonus, gate = _rwkv_features(
        shifted, w0_ref[...], a0_ref[...], kk_ref[...], ka_ref[...], rk_ref[...],
        w2_ref[...], a2_ref[...], g2_ref[...], same_head.astype(F32))

    am, bm, km, rm, vv, glast = [], [], [], [], [], []
    for ch in range(nchunk):
        sl = slice(ch * c, (ch + 1) * c)
        lw = lw_all[sl]
        l1 = lw.astype(BF16)
        r1 = lw - l1.astype(F32)
        l2 = r1.astype(BF16)
        l3 = (r1 - l2.astype(F32)).astype(BF16)
        tri_dot = lambda x: jnp.dot(ltri, x, preferred_element_type=F32)
        cum = tri_dot(l1) + (tri_dot(l2) + tri_dot(l3))
        g_in = jnp.exp(cum)
        g_inv = jnp.exp(-cum)
        am.append(kk_all[sl] * jnp.exp(cum - lw))
        bm.append(b_all[sl] * g_inv)
        km.append(k_all[sl] * g_inv)
        rm.append(r_all[sl] * g_in)
        vv.append(v_all[sl])
        glast.append(g_in[c - 1:c, :])

    lak, mrb, mrk, pw, tt = [], [], [], [], []
    for ch in range(nchunk):
        for h in range(2):
            msk = head0 if h == 0 else jnp.logical_not(head0)
            a_h = jnp.where(msk, am[ch], 0.0)
            r_h = jnp.where(msk, rm[ch], 0.0)
            lab = jnp.where(strict, _bdot(a_h, bm[ch], _NT), 0.0)
            lak.append(jnp.where(strict, _bdot(a_h, km[ch], _NT), 0.0))
            mrb.append(jnp.where(incl, _bdot(r_h, bm[ch], _NT), 0.0))
            mrk.append(jnp.where(incl, _bdot(r_h, km[ch], _NT), 0.0))
            pw.append(-lab)
            tt.append(eye - lab)
    for _ in range(max(1, int(math.ceil(math.log2(c))) - 1)):
        pw = [_bdot(p_, p_, _NN) for p_ in pw]
        tt = [t_ + _bdot(t_, p_, _NN) for t_, p_ in zip(tt, pw)]

    rp, y0, pmat, qmat = [], [], [], []
    for ch in range(nchunk):
        t0, t1 = tt[2 * ch], tt[2 * ch + 1]
        ta = sel(_bdot(t0, am[ch], _NN), _bdot(t1, am[ch], _NN))
        w = sel(_bdot(t0, _bdot(lak[2 * ch], vv[ch], _NN), _NN),
                _bdot(t1, _bdot(lak[2 * ch + 1], vv[ch], _NN), _NN))
        rp.append(rm[ch] - sel(_bdot(mrb[2 * ch], ta, _NN), _bdot(mrb[2 * ch + 1], ta, _NN)))
        y0.append(sel(_bdot(mrk[2 * ch], vv[ch], _NN) - _bdot(mrb[2 * ch], w, _NN),
                      _bdot(mrk[2 * ch + 1], vv[ch], _NN) - _bdot(mrb[2 * ch + 1], w, _NN)))
        pmat.append(jnp.where(same_head, -_bdot(ta, bm[ch], _TN), 0.0))
        qmat.append(jnp.where(same_head, _bdot(vv[ch], km[ch], _TN) - _bdot(w, bm[ch], _TN), 0.0))

    st = s_ref[...]
    ys = []
    for ch in range(nchunk):
        ys.append(_bdot(rp[ch], st, _NT) + y0[ch])
        st = (st + _bdot(st, pmat[ch], _NN) + qmat[ch]) * glast[ch]
    s_ref[...] = st

    y = jnp.concatenate(ys, axis=0)
    mu = _bdot(y, seg_mean, _NN)
    yc = y - mu
    var = _bdot(yc * yc, seg_mean, _NN)
    yn = yc * lax.rsqrt(var + GN_EPS) * lng_ref[...] + lnb_ref[...]
    y_ref[0] = ((yn + bonus) * gate).astype(y_ref.dtype)


def _rwkv_scan(proj, mu, w0, a0, k_k, k_a, r_k, w2p, a2p, g2, ln_g, ln_b):
    b, s, _ = proj.shape
    width = w0.shape[1]
    hp = width // LANES
    sb = min(SCAN_SB, s)
    cols = [lambda j: j, lambda j: hp + j, lambda j: 2 * hp + j,
            lambda j: 3 * hp, lambda j: 3 * hp + 1, lambda j: 3 * hp + 2]
    p_specs = [pl.BlockSpec((1, sb, LANES), lambda i, j, t, f=f: (i, t, f(j))) for f in cols]
    mu_specs = [pl.BlockSpec((1, LANES), lambda i, j, t, f=f: (0, f(j))) for f in cols]
    vec = pl.BlockSpec((1, LANES), lambda i, j, t: (0, j))
    mat = pl.BlockSpec((LANES, LANES), lambda i, j, t: (0, j))
    return pl.pallas_call(
        _scan_body,
        grid=(b, hp, s // sb),
        in_specs=p_specs + mu_specs + [vec] * 5 + [mat] * 3 + [vec, vec],
        out_specs=pl.BlockSpec((1, sb, LANES), lambda i, j, t: (i, t, j)),
        out_shape=jax.ShapeDtypeStruct((b, s, width), BF16),
        scratch_shapes=[pltpu.VMEM((LANES, LANES), F32), pltpu.VMEM((6, 8, LANES), F32)],
        compiler_params=_params("parallel", "parallel", "arbitrary"),
        name="rwkv_scan",
    )(*([proj] * 6), *([mu] * 6), w0, a0, k_k, k_a, r_k, w2p, a2p, g2, ln_g, ln_b)


def _layer_norm(y, g, b):
    mu = jnp.mean(y, axis=-1, keepdims=True)
    yc = y - mu
    var = jnp.mean(yc * yc, axis=-1, keepdims=True)
    return yc * lax.rsqrt(var + LN_EPS) * g + b


def _merge_body(alpha, x_ref, yf_ref, yr_ref, gf_ref, gr_ref, pf_ref, pr_ref, wo_ref, g_ref, b_ref, o_ref):
    mf = jnp.dot(yf_ref[...], pf_ref[...], preferred_element_type=F32)
    mr = jnp.dot(yr_ref[...], pr_ref[...], preferred_element_type=F32)
    merged = _sigmoid(gf_ref[...]) * mf + _sigmoid(gr_ref[...]) * mr
    out = jnp.dot(merged.astype(BF16), wo_ref[...], preferred_element_type=F32)
    o_ref[...] = _layer_norm(alpha * x_ref[...] + out, g_ref[...], b_ref[...])


def _merge(alpha, x2, y_fox, y_rwkv, gates, p_fox, p_rwkv, w_o, ln_g, ln_b):
    n, d = x2.shape
    wf = y_fox.shape[1]
    wr = y_rwkv.shape[1]
    tm = min(MERGE_TM, n)
    row = lambda w, c=0: pl.BlockSpec((tm, w), lambda i: (i, c))
    full = lambda a: pl.BlockSpec(a.shape, lambda i: (0, 0))
    return pl.pallas_call(
        functools.partial(_merge_body, alpha),
        grid=(n // tm,),
        in_specs=[row(d), row(wf), row(wr), row(d, 0), row(d, 1),
                  full(p_fox), full(p_rwkv), full(w_o), full(ln_g), full(ln_b)],
        out_specs=row(d),
        out_shape=jax.ShapeDtypeStruct((n, d), F32),
        compiler_params=_params("parallel"),
        name="merge_ln",
    )(x2, y_fox, y_rwkv, gates, gates, p_fox, p_rwkv, w_o, ln_g, ln_b)


def _topk_rows(vals, k, payload=None):
    n, t = vals.shape
    rio = lax.broadcasted_iota(I32, (n, t), 0).astype(F32)
    kio = lax.broadcasted_iota(I32, (k, t), 0)
    out_v = jnp.zeros((k, t), F32)
    out_i = jnp.zeros((k, t), I32)
    cur = vals
    for it in range(k):
        mx = jnp.max(cur, axis=0, keepdims=True)
        sel = jnp.min(jnp.where(cur == mx, rio, float(n)), axis=0, keepdims=True)
        hit = rio == sel
        if payload is None:
            got = sel.astype(I32)
        else:
            got = jnp.sum(jnp.where(hit, payload, 0), axis=0, keepdims=True)
        out_v = jnp.where(kio == it, mx, out_v)
        out_i = jnp.where(kio == it, got, out_i)
        cur = jnp.where(hit, -jnp.inf, cur)
    return out_v, out_i


def _pair_candidates(top0, top1, combine):
    assert top0.shape[0] == 16 and top1.shape[0] == 16
    sub = lax.broadcasted_iota(I32, (8, top0.shape[1]), 0)
    lo = top1[0:8]
    lo_up = pltpu.roll(lo, 4, 0)
    pieces = [combine(top0[0:1], top1)]
    pieces += [combine(top0[a:a + 1], lo) for a in (1, 2, 3)]
    pieces += [jnp.where(sub < 4, combine(top0[a:a + 1], lo), combine(top0[a + 1:a + 2], lo_up))
               for a in (4, 6)]
    pieces.append(combine(top0[8:16], top1[0:1]))
    return jnp.concatenate(pieces, axis=0)


def _route_body(n_keys, x_ref, wq_ref, keys_ref, idx_ref, gate_ref):
    heads = keys_ref.shape[0] // 2
    half = keys_ref.shape[2]
    kt = PEER_TOPK
    q = jnp.dot(x_ref[...].astype(BF16), wq_ref[...], preferred_element_type=F32)
    idx_rows, gate_rows = [], []
    for h in range(heads):
        tops = []
        for p in range(2):
            c = 2 * h + p
            qc = q[:, c * half:(c + 1) * half].astype(BF16)
            s_t = _nt(keys_ref[c], qc)
            tops.append(_topk_rows(s_t, kt))
        (s0, i0), (s1, i1) = tops
        cand_s = _pair_candidates(s0, s1, lambda x, y: x + y)
        cand_i = _pair_candidates(i0, i1, lambda x, y: x * n_keys + y)
        best_s, best_i = _topk_rows(cand_s, kt, cand_i)
        e = jnp.exp(best_s - best_s[0:1, :])
        gate_rows.append(e / jnp.sum(e, axis=0, keepdims=True))
        idx_rows.append(best_i)
    idx_ref[...] = jnp.concatenate(idx_rows, axis=0).T
    gate_ref[...] = jnp.concatenate(gate_rows, axis=0).T


def _peer_route(x1, w_q, keys):
    n, d = x1.shape
    n_keys = keys.shape[1]
    slots = (keys.shape[0] // 2) * PEER_TOPK
    tm = min(ROUTE_TM, n)
    return pl.pallas_call(
        functools.partial(_route_body, n_keys),
        grid=(n // tm,),
        in_specs=[pl.BlockSpec((tm, d), lambda i: (i, 0)),
                  pl.BlockSpec(w_q.shape, lambda i: (0, 0)),
                  pl.BlockSpec(keys.shape, lambda i: (0, 0, 0))],
        out_specs=[pl.BlockSpec((tm, slots), lambda i: (i, 0))] * 2,
        out_shape=[jax.ShapeDtypeStruct((n, slots), I32), jax.ShapeDtypeStruct((n, slots), F32)],
        compiler_params=_params("parallel"),
        name="peer_route",
    )(x1, w_q, keys)


def _gelu(x):
    return 0.5 * x * (1.0 + lax.erf(x * (2.0 ** -0.5)))


def _pack_tables(u, v):
    e, d = u.shape
    return jnp.concatenate([u.reshape(e, d // LANES, LANES), v.reshape(e, d // LANES, LANES)],
                           axis=1).reshape(-1, LANES)


def _gather_body(alpha, x_ref, gate_ref, idx_hbm, tab_hbm, g_ref, b_ref, o_ref, idx_s, buf, sem_i, sem):
    ts, slots = gate_ref.shape
    d = x_ref.shape[1]
    nw = d // LANES
    erows = 2 * nw
    tb = GATHER_TB
    rows = tb * slots
    nsub = ts // tb
    step = pl.program_id(0)
    nsteps = pl.num_programs(0)
    ib = step % 2
    nbuf = GATHER_AHEAD + 1
    kgrp = slots // (2 * nw)

    def idx_copy(s, b):
        return pltpu.make_async_copy(idx_hbm.at[pl.ds(s * ts, ts), :], idx_s.at[b], sem_i.at[b])

    def issue(b, row, t, k0, slot):
        for k in range(k0, k0 + kgrp):
            e = idx_s[b, row, k]
            src = tab_hbm.at[pl.ds(pl.multiple_of(e * erows, erows), erows), :]
            dst = buf.at[slot, pl.ds((t * slots + k) * GATHER_PITCH, erows), :]
            pltpu.make_async_copy(src, dst, sem.at[slot]).start(priority=k % 2)

    def wait(slot):
        pltpu.make_async_copy(tab_hbm.at[pl.ds(0, rows * erows), :],
                              buf.at[slot, pl.ds(0, rows * erows), :], sem.at[slot]).wait()

    @pl.when(step == 0)
    def _():
        idx_copy(0, 0).start()
        idx_copy(0, 0).wait()
        for j in range(GATHER_AHEAD):
            for t in range(tb):
                for k0 in range(0, slots, kgrp):
                    issue(0, j * tb + t, t, k0, j)

    @pl.when(step + 1 < nsteps)
    def _():
        idx_copy(step + 1, 1 - ib).start()

    eye = (lax.broadcasted_iota(I32, (slots, slots), 0)
           == lax.broadcasted_iota(I32, (slots, slots), 1)).astype(F32)
    rio = lax.broadcasted_iota(I32, (tb, 1), 0)
    tio = lax.broadcasted_iota(I32, (slots, tb), 1)

    def compute(j, slot, nb, nj, nslot):
        r0 = pl.multiple_of(j * tb, tb)
        xb = x_ref[pl.ds(r0, tb), :]
        gcol = _nt(eye, gate_ref[pl.ds(r0, tb), :], HI)
        chunk = lambda t, c: buf[slot, pl.ds(t * slots * GATHER_PITCH + c, slots, stride=GATHER_PITCH), :]
        hmat = jnp.zeros((slots, tb), F32)
        for t in range(tb):
            pu = None
            for c in range(nw):
                issue(nb, nj * tb + t, t, c * kgrp, nslot)
                term = chunk(t, c) * xb[t:t + 1, c * LANES:(c + 1) * LANES]
                pu = term if pu is None else pu + term
            hmat = jnp.where(tio == t, jnp.sum(pu, axis=1, keepdims=True), hmat)
        wmat = _gelu(hmat) * gcol
        acc = jnp.zeros(xb.shape, F32)
        for t in range(tb):
            w = wmat[:, t:t + 1]
            outs = []
            for c in range(nw):
                issue(nb, nj * tb + t, t, (nw + c) * kgrp, nslot)
                outs.append(jnp.sum(chunk(t, nw + c) * w, axis=0, keepdims=True))
            acc = jnp.where(rio == t, jnp.concatenate(outs, axis=1), acc)
        o_ref[pl.ds(r0, tb), :] = _layer_norm(alpha * xb + acc, g_ref[...], b_ref[...])

    more = step + 1 < nsteps
    g0 = step * nsub

    def sub(j, carry):
        @pl.when(jnp.logical_and(j == nsub - GATHER_AHEAD, more))
        def _():
            idx_copy(step + 1, 1 - ib).wait()

        slot = (g0 + j) % nbuf
        wait(slot)
        jn = j + GATHER_AHEAD
        over = jn >= nsub
        compute(j, slot, jnp.where(jnp.logical_and(over, more), 1 - ib, ib), jnp.where(over, jn - nsub, jn),
                (g0 + jn) % nbuf)
        return carry

    lax.fori_loop(0, nsub, sub, 0)

    @pl.when(jnp.logical_not(more))
    def _():
        for a in range(GATHER_AHEAD):
            wait((g0 + nsub + a) % nbuf)


def _peer_gather(alpha, x1, gates, idx, tab, ln_g, ln_b):
    n, d = x1.shape
    slots = gates.shape[1]
    ts = min(GATHER_TS, n)
    rows = GATHER_TB * slots
    any_spec = pl.BlockSpec(memory_space=pl.ANY)
    return pl.pallas_call(
        functools.partial(_gather_body, alpha),
        grid=(n // ts,),
        in_specs=[pl.BlockSpec((ts, d), lambda i: (i, 0)),
                  pl.BlockSpec((ts, slots), lambda i: (i, 0)),
                  any_spec, any_spec,
                  pl.BlockSpec((1, d), lambda i: (0, 0)),
                  pl.BlockSpec((1, d), lambda i: (0, 0))],
        out_specs=pl.BlockSpec((ts, d), lambda i: (i, 0)),
        out_shape=jax.ShapeDtypeStruct((n, d), F32),
        scratch_shapes=[pltpu.SMEM((2, ts, slots), I32),
                        pltpu.VMEM((GATHER_AHEAD + 1, rows * GATHER_PITCH, LANES), F32),
                        pltpu.SemaphoreType.DMA((2,)),
                        pltpu.SemaphoreType.DMA((GATHER_AHEAD + 1,))],
        compiler_params=_params("arbitrary"),
        name="peer_gather",
    )(x1, gates, idx, tab, ln_g, ln_b)


def _pad_rows(w, n):
    return jnp.pad(w, ((0, n - w.shape[0]), (0, 0)))


def kernel(x, w_in, fox_f_bias, rwkv_mu, rwkv_w0, rwkv_w2, rwkv_a0, rwkv_a2, rwkv_g2, rwkv_k_k, rwkv_k_a, rwkv_r_k, rwkv_ln_g, rwkv_ln_b, p_fox, p_rwkv, w_o, ln1_g, ln1_b, peer_w_q, peer_sub_keys, peer_u, peer_v, ln2_g, ln2_b):
    bsz, seq, d = x.shape
    depth = w_in.shape[0]
    n = bsz * seq
    fox_heads = fox_f_bias.shape[1]
    fw = p_fox.shape[1]
    rw = p_rwkv.shape[1]
    w_lora, a_lora, g_lora = rwkv_w2.shape[1], rwkv_a2.shape[1], rwkv_g2.shape[1]
    fox_cols = 3 * fw + fox_heads
    rwkv_cols = 3 * rw + w_lora + a_lora + g_lora
    alpha = (2 * depth) ** 0.25
    hp = fw // LANES

    xcur = x.reshape(n, d)
    for l in range(depth):
        w = w_in[l]
        w_qkv = w[:, :3 * fw].astype(BF16)
        wf_t = w[:, 3 * fw:fox_cols].T.astype(BF16)
        wr = w[:, fox_cols:fox_cols + rwkv_cols]
        zpad = lambda m, c: jnp.pad(m, ((0, 0), (0, c - m.shape[1])))
        o_w = 3 * rw
        w_rw = jnp.concatenate([wr[:, :o_w],
                                zpad(wr[:, o_w:o_w + w_lora], LANES),
                                zpad(wr[:, o_w + w_lora:o_w + w_lora + a_lora], LANES),
                                zpad(wr[:, o_w + w_lora + a_lora:], LANES)], axis=1).astype(BF16)
        mu = rwkv_mu[l]
        mu_p = jnp.concatenate([mu[:o_w],
                                jnp.pad(mu[o_w:o_w + w_lora], (0, LANES - w_lora)),
                                jnp.pad(mu[o_w + w_lora:o_w + w_lora + a_lora], (0, LANES - a_lora)),
                                jnp.pad(mu[o_w + w_lora + a_lora:], (0, LANES - g_lora))])[None, :]
        w_gate = w[:, fox_cols + rwkv_cols:].astype(BF16)

        qkv, rwp, gates = _project(xcur, w_qkv, w_rw, w_gate)

        c_row = _forget_cumsum(xcur.reshape(bsz, seq, d), wf_t, fox_f_bias[l][:, None])
        c_row = c_row.reshape(bsz, hp, 2, seq)
        c_col = jnp.swapaxes(c_row, 2, 3)
        y_fox = _fox_attention(qkv.reshape(bsz, seq, 3 * fw), c_col, c_row, bsz, seq, fw)

        row = lambda a: a[l][None, :]
        y_rwkv = _rwkv_scan(rwp.reshape(bsz, seq, -1), mu_p, row(rwkv_w0), row(rwkv_a0), row(rwkv_k_k),
                            row(rwkv_k_a), row(rwkv_r_k), _pad_rows(rwkv_w2[l], LANES),
                            _pad_rows(rwkv_a2[l], LANES), rwkv_g2[l], row(rwkv_ln_g), row(rwkv_ln_b))

        x1 = _merge(alpha, xcur, y_fox.reshape(n, fw), y_rwkv.reshape(n, rw), gates,
                    p_fox[l].astype(BF16), p_rwkv[l].astype(BF16), w_o[l].astype(BF16),
                    row(ln1_g), row(ln1_b))

        keys = peer_sub_keys[l]
        n_keys, half = keys.shape[2], keys.shape[3]
        idx, pg = _peer_route(x1, peer_w_q[l].astype(BF16),
                              keys.reshape(-1, n_keys, half).astype(BF16))
        xcur = _peer_gather(alpha, x1, pg, idx, _pack_tables(peer_u[l], peer_v[l]), row(ln2_g), row(ln2_b))
    return xcur.reshape(bsz, seq, d)
```

```python
import functools
import math

import jax
import jax.numpy as jnp
from jax import lax
from jax.experimental import pallas as pl
from jax.experimental.pallas import tpu as pltpu

F32 = jnp.float32
BF16 = jnp.bfloat16
I32 = jnp.int32
HI = lax.Precision.HIGHEST

LANES = 128
HEAD_DIM = 64
PEER_TOPK = 16
LN_EPS = 1e-5
GN_EPS = 64e-5
NEG_BIG = -1e30
VMEM_LIMIT = 56 * 1024 * 1024

MM_TM = 256
FG_L = 512
ATT_BQ = 1024
ATT_RB = 256
SCAN_SB = 1024
SCAN_C = 64
MERGE_TM = 512
ROUTE_TM = 256
GATHER_TS = 128
GATHER_TB = 8
GATHER_AHEAD = 2
GATHER_SPLIT = 2
GATHER_EMIT = (0, 4, 0, 4)
GATHER_PITCH = 20


def _nt(a, b, precision=None):
    return lax.dot_general(a, b, (((1,), (1,)), ((), ())), precision=precision,
                           preferred_element_type=F32)


_NN = ((1,), (0,))
_NT = ((1,), (1,))
_TN = ((0,), (0,))


def _bdot(a, b, dims):
    return lax.dot_general(a.astype(BF16), b.astype(BF16), (dims, ((), ())), preferred_element_type=F32)


def _sigmoid(x):
    return 1.0 / (1.0 + jnp.exp(-x))


def _params(*sem):
    return pltpu.CompilerParams(dimension_semantics=sem, vmem_limit_bytes=VMEM_LIMIT)


def _proj_body(x_ref, wq_ref, wr_ref, wg_ref, q_o, r_o, g_o):
    xb = x_ref[...].astype(BF16)
    q_o[...] = jnp.dot(xb, wq_ref[...], preferred_element_type=F32).astype(q_o.dtype)
    r_o[...] = jnp.dot(xb, wr_ref[...], preferred_element_type=F32)
    g_o[...] = jnp.dot(xb, wg_ref[...], preferred_element_type=F32)


def _project(x, w_qkv, w_rw, w_gate):
    m, k = x.shape
    tm = min(MM_TM, m)
    full = lambda w: pl.BlockSpec(w.shape, lambda i: (0, 0))
    out = lambda w: pl.BlockSpec((tm, w.shape[1]), lambda i: (i, 0))
    return pl.pallas_call(
        _proj_body,
        grid=(m // tm,),
        in_specs=[pl.BlockSpec((tm, k), lambda i: (i, 0)), full(w_qkv), full(w_rw), full(w_gate)],
        out_specs=[out(w_qkv), out(w_rw), out(w_gate)],
        out_shape=[jax.ShapeDtypeStruct((m, w_qkv.shape[1]), BF16),
                   jax.ShapeDtypeStruct((m, w_rw.shape[1]), F32),
                   jax.ShapeDtypeStruct((m, w_gate.shape[1]), F32)],
        compiler_params=_params("parallel"),
        name="in_proj",
    )(x, w_qkv, w_rw, w_gate)


def _fgate_body(x_ref, wf_ref, b_ref, c_ref, carry_ref):
    @pl.when(pl.program_id(1) == 0)
    def _():
        carry_ref[...] = jnp.zeros_like(carry_ref)

    xb = x_ref[0].astype(BF16)
    f = _nt(wf_ref[...], xb) + b_ref[...]
    logf = -(jnp.maximum(-f, 0.0) + jnp.log1p(jnp.exp(-jnp.abs(f))))
    n = f.shape[1]
    tri = (lax.broadcasted_iota(I32, (n, n), 0) <= lax.broadcasted_iota(I32, (n, n), 1)).astype(F32)
    cs = jnp.dot(logf, tri, precision=HI, preferred_element_type=F32) + carry_ref[:, 0:1]
    c_ref[0] = cs
    carry_ref[...] = jnp.broadcast_to(cs[:, n - 1:n], carry_ref.shape)


def _forget_cumsum(x3, wf_t, bias):
    b, s, d = x3.shape
    h = wf_t.shape[0]
    blk = min(FG_L, s)
    return pl.pallas_call(
        _fgate_body,
        grid=(b, s // blk),
        in_specs=[pl.BlockSpec((1, blk, d), lambda i, j: (i, j, 0)),
                  pl.BlockSpec((h, d), lambda i, j: (0, 0)),
                  pl.BlockSpec((h, 1), lambda i, j: (0, 0))],
        out_specs=pl.BlockSpec((1, h, blk), lambda i, j: (i, 0, j)),
        out_shape=jax.ShapeDtypeStruct((b, h, s), F32),
        scratch_shapes=[pltpu.VMEM((h, LANES), F32)],
        compiler_params=_params("parallel", "arbitrary"),
        name="forget_cumsum",
    )(x3, wf_t, bias)


def _fox_body(qi_tab, ki_tab, q_ref, k_ref, v_ref, cq_ref, ck_ref, o_ref,
              qh_s, m_s, l_s, acc_s):
    p = pl.program_id(2)
    qi = qi_tab[p]
    ki = ki_tab[p]
    bq = q_ref.shape[1]
    bk = k_ref.shape[1]
    lane = lax.broadcasted_iota(I32, (1, LANES), 1)

    @pl.when(ki == 0)
    def _():
        q = q_ref[0] * jnp.asarray(HEAD_DIM ** -0.5, BF16)
        zero = jnp.zeros_like(q)
        qh_s[0] = jnp.where(lane < HEAD_DIM, q, zero)
        qh_s[1] = jnp.where(lane >= HEAD_DIM, q, zero)
        m_s[...] = jnp.full_like(m_s, NEG_BIG)
        l_s[...] = jnp.zeros_like(l_s)
        acc_s[...] = jnp.zeros_like(acc_s)

    def step(masked):
        rb = min(ATT_RB, bq)
        units = [(r, h) for r in range(bq // rb) for h in range(2)]

        def scores(unit):
            r, h = unit
            z = _nt(qh_s[h, r * rb:(r + 1) * rb, :], k_ref[0]) - ck_ref[0, 0][h:h + 1, :]
            if masked:
                rows = lax.broadcasted_iota(I32, (rb, bk), 0) + r * rb
                cols = lax.broadcasted_iota(I32, (rb, bk), 1)
                z = jnp.where(rows >= cols, z, NEG_BIG)
            return z

        def finish(unit, z):
            r, h = unit
            rs = slice(r * rb, (r + 1) * rb)
            cq = cq_ref[0, 0, rs, :][:, h:h + 1]
            m_prev = m_s[h, rs, :]
            m_new = jnp.maximum(m_prev, jnp.max(z, axis=1, keepdims=True) + cq)
            alpha = jnp.exp(m_prev - m_new)
            pr = jnp.exp(z - (m_new - cq))
            l_s[h, rs, :] = alpha * l_s[h, rs, :] + jnp.sum(pr, axis=1, keepdims=True)
            acc_s[h, rs, :] = alpha * acc_s[h, rs, :] + jnp.dot(
                pr.astype(BF16), v_ref[0], preferred_element_type=F32)
            m_s[h, rs, :] = m_new

        z = scores(units[0])
        for i, unit in enumerate(units):
            z_next = scores(units[i + 1]) if i + 1 < len(units) else None
            finish(unit, z)
            z = z_next

    @pl.when(ki < qi)
    def _():
        step(False)

    @pl.when(ki == qi)
    def _():
        step(True)
        o0 = acc_s[0] / l_s[0]
        o1 = acc_s[1] / l_s[1]
        o_ref[0] = jnp.where(lane < HEAD_DIM, o0, o1).astype(o_ref.dtype)


def _fox_attention(qkv, c_col, c_row, b, s, width):
    hp = width // LANES
    bq = min(ATT_BQ, s)
    nq = s // bq
    pairs = [(i, j) for i in range(nq) for j in range(i + 1)]
    qi_tab = jnp.asarray([pq for pq, _ in pairs], I32)
    ki_tab = jnp.asarray([pk for _, pk in pairs], I32)
    grid_spec = pltpu.PrefetchScalarGridSpec(
        num_scalar_prefetch=2,
        grid=(b, hp, len(pairs)),
        in_specs=[
            pl.BlockSpec((1, bq, LANES), lambda i, j, p, qt, kt: (i, qt[p], j)),
            pl.BlockSpec((1, bq, LANES), lambda i, j, p, qt, kt: (i, kt[p], hp + j)),
            pl.BlockSpec((1, bq, LANES), lambda i, j, p, qt, kt: (i, kt[p], 2 * hp + j)),
            pl.BlockSpec((1, 1, bq, 2), lambda i, j, p, qt, kt: (i, j, qt[p], 0)),
            pl.BlockSpec((1, 1, 2, bq), lambda i, j, p, qt, kt: (i, j, 0, kt[p])),
        ],
        out_specs=pl.BlockSpec((1, bq, LANES), lambda i, j, p, qt, kt: (i, qt[p], j)),
        scratch_shapes=[pltpu.VMEM((2, bq, LANES), BF16),
                        pltpu.VMEM((2, bq, 1), F32),
                        pltpu.VMEM((2, bq, 1), F32),
                        pltpu.VMEM((2, bq, LANES), F32)],
    )
    return pl.pallas_call(
        _fox_body,
        grid_spec=grid_spec,
        out_shape=jax.ShapeDtypeStruct((b, s, width), BF16),
        compiler_params=_params("parallel", "parallel", "arbitrary"),
        name="fox_attention",
    )(qi_tab, ki_tab, qkv, qkv, qkv, c_col, c_row)


def _rwkv_features(shifted, w0, a0, k_k, k_a, r_k, w2, a2, g2, seg):
    r, k, v, wl, al, gl = shifted
    hdot = lambda x, y: _bdot(x, y, _NN)
    w_pre = w0 + hdot(jnp.tanh(wl), w2)
    lw = (-math.exp(-0.5)) * _sigmoid(w_pre)
    a = _sigmoid(a0 + hdot(al, a2))
    g = hdot(_sigmoid(gl), g2)
    kk = k * k_k
    kkn = kk / jnp.maximum(jnp.sqrt(hdot(kk * kk, seg)), 1e-12)
    kmod = k * (1.0 + (a - 1.0) * k_a)
    bonus = hdot(r * kmod * r_k, seg) * v
    return r, lw, kmod, v, kkn, kkn * a, bonus, g


def _scan_body(*refs):
    p_refs, mu_refs = refs[0:6], refs[6:12]
    w0_ref, a0_ref, kk_ref, ka_ref, rk_ref, w2_ref, a2_ref, g2_ref, lng_ref, lnb_ref = refs[12:22]
    y_ref, s_ref, carry_ref = refs[22:25]
    sb = y_ref.shape[1]
    c = min(SCAN_C, sb)

    @pl.when(pl.program_id(2) == 0)
    def _():
        s_ref[...] = jnp.zeros_like(s_ref)
        carry_ref[...] = jnp.zeros_like(carry_ref)

    lane = lax.broadcasted_iota(I32, (1, LANES), 1)
    head0 = lane < HEAD_DIM
    ti = lax.broadcasted_iota(I32, (c, c), 0)
    si = lax.broadcasted_iota(I32, (c, c), 1)
    incl = ti >= si
    strict = ti > si
    ltri = incl.astype(BF16)
    eye = (ti == si).astype(F32)
    ri = lax.broadcasted_iota(I32, (LANES, LANES), 0)
    ci = lax.broadcasted_iota(I32, (LANES, LANES), 1)
    same_head = (ri < HEAD_DIM) == (ci < HEAD_DIM)
    seg_mean = same_head.astype(F32) * (1.0 / HEAD_DIM)
    nchunk = sb // c
    sel = lambda x0, x1: jnp.where(head0, x0, x1)


    row0 = lax.broadcasted_iota(I32, (sb, LANES), 0) == 0
    shifted = []
    for i in range(6):
        p = p_refs[i][0]
        prev = jnp.where(row0, jnp.broadcast_to(carry_ref[i, 0:1, :], p.shape), pltpu.roll(p, 1, 0))
        carry_ref[i] = jnp.broadcast_to(p[sb - 1:sb, :], carry_ref.shape[1:])
        shifted.append(p + mu_refs[i][...] * (prev - p))
    r_all, lw_all, k_all, v_all, kk_all, b_all, bonus, gate = _rwkv_features(
        shifted, w0_ref[...], a0_ref[...], kk_ref[...], ka_ref[...], rk_ref[...],
        w2_ref[...], a2_ref[...], g2_ref[...], same_head.astype(F32))

    am, bm, km, rm, vv, glast = [], [], [], [], [], []
    for ch in range(nchunk):
        sl = slice(ch * c, (ch + 1) * c)
        lw = lw_all[sl]
        l1 = lw.astype(BF16)
        r1 = lw - l1.astype(F32)
        l2 = r1.astype(BF16)
        l3 = (r1 - l2.astype(F32)).astype(BF16)
        tri_dot = lambda x: jnp.dot(ltri, x, preferred_element_type=F32)
        cum = tri_dot(l1) + (tri_dot(l2) + tri_dot(l3))
        g_in = jnp.exp(cum)
        g_inv = jnp.exp(-cum)
        am.append(kk_all[sl] * jnp.exp(cum - lw))
        bm.append(b_all[sl] * g_inv)
        km.append(k_all[sl] * g_inv)
        rm.append(r_all[sl] * g_in)
        vv.append(v_all[sl])
        glast.append(g_in[c - 1:c, :])

    lak, mrb, mrk, pw, tt = [], [], [], [], []
    for ch in range(nchunk):
        for h in range(2):
            msk = head0 if h == 0 else jnp.logical_not(head0)
            a_h = jnp.where(msk, am[ch], 0.0)
            r_h = jnp.where(msk, rm[ch], 0.0)
            lab = jnp.where(strict, _bdot(a_h, bm[ch], _NT), 0.0)
            lak.append(jnp.where(strict, _bdot(a_h, km[ch], _NT), 0.0))
            mrb.append(jnp.where(incl, _bdot(r_h, bm[ch], _NT), 0.0))
            mrk.append(jnp.where(incl, _bdot(r_h, km[ch], _NT), 0.0))
            pw.append(-lab)
            tt.append(eye - lab)
    for _ in range(max(1, int(math.ceil(math.log2(c))) - 1)):
        pw = [_bdot(p_, p_, _NN) for p_ in pw]
        tt = [t_ + _bdot(t_, p_, _NN) for t_, p_ in zip(tt, pw)]

    rp, y0, pmat, qmat = [], [], [], []
    for ch in range(nchunk):
        t0, t1 = tt[2 * ch], tt[2 * ch + 1]
        ta = sel(_bdot(t0, am[ch], _NN), _bdot(t1, am[ch], _NN))
        w = sel(_bdot(t0, _bdot(lak[2 * ch], vv[ch], _NN), _NN),
                _bdot(t1, _bdot(lak[2 * ch + 1], vv[ch], _NN), _NN))
        rp.append(rm[ch] - sel(_bdot(mrb[2 * ch], ta, _NN), _bdot(mrb[2 * ch + 1], ta, _NN)))
        y0.append(sel(_bdot(mrk[2 * ch], vv[ch], _NN) - _bdot(mrb[2 * ch], w, _NN),
                      _bdot(mrk[2 * ch + 1], vv[ch], _NN) - _bdot(mrb[2 * ch + 1], w, _NN)))
        pmat.append(jnp.where(same_head, -_bdot(ta, bm[ch], _TN), 0.0))
        qmat.append(jnp.where(same_head, _bdot(vv[ch], km[ch], _TN) - _bdot(w, bm[ch], _TN), 0.0))

    st = s_ref[...]
    ys = []
    for ch in range(nchunk):
        ys.append(_bdot(rp[ch], st, _NT) + y0[ch])
        st = (st + _bdot(st, pmat[ch], _NN) + qmat[ch]) * glast[ch]
    s_ref[...] = st

    y = jnp.concatenate(ys, axis=0)
    mu = _bdot(y, seg_mean, _NN)
    yc = y - mu
    var = _bdot(yc * yc, seg_mean, _NN)
    yn = yc * lax.rsqrt(var + GN_EPS) * lng_ref[...] + lnb_ref[...]
    y_ref[0] = ((yn + bonus) * gate).astype(y_ref.dtype)


def _rwkv_scan(proj, mu, w0, a0, k_k, k_a, r_k, w2p, a2p, g2, ln_g, ln_b):
    b, s, _ = proj.shape
    width = w0.shape[1]
    hp = width // LANES
    sb = min(SCAN_SB, s)
    cols = [lambda j: j, lambda j: hp + j, lambda j: 2 * hp + j,
            lambda j: 3 * hp, lambda j: 3 * hp + 1, lambda j: 3 * hp + 2]
    p_specs = [pl.BlockSpec((1, sb, LANES), lambda i, j, t, f=f: (i, t, f(j))) for f in cols]
    mu_specs = [pl.BlockSpec((1, LANES), lambda i, j, t, f=f: (0, f(j))) for f in cols]
    vec = pl.BlockSpec((1, LANES), lambda i, j, t: (0, j))
    mat = pl.BlockSpec((LANES, LANES), lambda i, j, t: (0, j))
    return pl.pallas_call(
        _scan_body,
        grid=(b, hp, s // sb),
        in_specs=p_specs + mu_specs + [vec] * 5 + [mat] * 3 + [vec, vec],
        out_specs=pl.BlockSpec((1, sb, LANES), lambda i, j, t: (i, t, j)),
        out_shape=jax.ShapeDtypeStruct((b, s, width), BF16),
        scratch_shapes=[pltpu.VMEM((LANES, LANES), F32), pltpu.VMEM((6, 8, LANES), F32)],
        compiler_params=_params("parallel", "parallel", "arbitrary"),
        name="rwkv_scan",
    )(*([proj] * 6), *([mu] * 6), w0, a0, k_k, k_a, r_k, w2p, a2p, g2, ln_g, ln_b)


def _layer_norm(y, g, b):
    mu = jnp.mean(y, axis=-1, keepdims=True)
    yc = y - mu
    var = jnp.mean(yc * yc, axis=-1, keepdims=True)
    return yc * lax.rsqrt(var + LN_EPS) * g + b


def _merge_body(alpha, x_ref, yf_ref, yr_ref, gf_ref, gr_ref, pf_ref, pr_ref, wo_ref, g_ref, b_ref, o_ref):
    mf = jnp.dot(yf_ref[...], pf_ref[...], preferred_element_type=F32)
    mr = jnp.dot(yr_ref[...], pr_ref[...], preferred_element_type=F32)
    merged = _sigmoid(gf_ref[...]) * mf + _sigmoid(gr_ref[...]) * mr
    out = jnp.dot(merged.astype(BF16), wo_ref[...], preferred_element_type=F32)
    o_ref[...] = _layer_norm(alpha * x_ref[...] + out, g_ref[...], b_ref[...])


def _merge(alpha, x2, y_fox, y_rwkv, gates, p_fox, p_rwkv, w_o, ln_g, ln_b):
    n, d = x2.shape
    wf = y_fox.shape[1]
    wr = y_rwkv.shape[1]
    tm = min(MERGE_TM, n)
    row = lambda w, c=0: pl.BlockSpec((tm, w), lambda i: (i, c))
    full = lambda a: pl.BlockSpec(a.shape, lambda i: (0, 0))
    return pl.pallas_call(
        functools.partial(_merge_body, alpha),
        grid=(n // tm,),
        in_specs=[row(d), row(wf), row(wr), row(d, 0), row(d, 1),
                  full(p_fox), full(p_rwkv), full(w_o), full(ln_g), full(ln_b)],
        out_specs=row(d),
        out_shape=jax.ShapeDtypeStruct((n, d), F32),
        compiler_params=_params("parallel"),
        name="merge_ln",
    )(x2, y_fox, y_rwkv, gates, gates, p_fox, p_rwkv, w_o, ln_g, ln_b)


def _topk_rows(vals, k, payload=None):
    n, t = vals.shape
    rio = lax.broadcasted_iota(I32, (n, t), 0).astype(F32)
    kio = lax.broadcasted_iota(I32, (k, t), 0)
    out_v = jnp.zeros((k, t), F32)
    out_i = jnp.zeros((k, t), I32)
    cur = vals
    for it in range(k):
        mx = jnp.max(cur, axis=0, keepdims=True)
        sel = jnp.min(jnp.where(cur == mx, rio, float(n)), axis=0, keepdims=True)
        hit = rio == sel
        if payload is None:
            got = sel.astype(I32)
        else:
            got = jnp.sum(jnp.where(hit, payload, 0), axis=0, keepdims=True)
        out_v = jnp.where(kio == it, mx, out_v)
        out_i = jnp.where(kio == it, got, out_i)
        cur = jnp.where(hit, -jnp.inf, cur)
    return out_v, out_i


def _pair_candidates(top0, top1, combine):
    assert top0.shape[0] == 16 and top1.shape[0] == 16
    sub = lax.broadcasted_iota(I32, (8, top0.shape[1]), 0)
    lo = top1[0:8]
    lo_up = pltpu.roll(lo, 4, 0)
    pieces = [combine(top0[0:1], top1)]
    pieces += [combine(top0[a:a + 1], lo) for a in (1, 2, 3)]
    pieces += [jnp.where(sub < 4, combine(top0[a:a + 1], lo), combine(top0[a + 1:a + 2], lo_up))
               for a in (4, 6)]
    pieces.append(combine(top0[8:16], top1[0:1]))
    return jnp.concatenate(pieces, axis=0)


def _route_body(n_keys, x_ref, wq_ref, keys_ref, idx_ref, gate_ref):
    heads = keys_ref.shape[0] // 2
    half = keys_ref.shape[2]
    kt = PEER_TOPK
    q = jnp.dot(x_ref[...].astype(BF16), wq_ref[...], preferred_element_type=F32)
    idx_rows, gate_rows = [], []
    for h in range(heads):
        tops = []
        for p in range(2):
            c = 2 * h + p
            qc = q[:, c * half:(c + 1) * half].astype(BF16)
            s_t = _nt(keys_ref[c], qc)
            tops.append(_topk_rows(s_t, kt))
        (s0, i0), (s1, i1) = tops
        cand_s = _pair_candidates(s0, s1, lambda x, y: x + y)
        cand_i = _pair_candidates(i0, i1, lambda x, y: x * n_keys + y)
        best_s, best_i = _topk_rows(cand_s, kt, cand_i)
        e = jnp.exp(best_s - best_s[0:1, :])
        gate_rows.append(e / jnp.sum(e, axis=0, keepdims=True))
        idx_rows.append(best_i)
    idx_ref[...] = jnp.concatenate(idx_rows, axis=0).T
    gate_ref[...] = jnp.concatenate(gate_rows, axis=0).T


def _peer_route(x1, w_q, keys):
    n, d = x1.shape
    n_keys = keys.shape[1]
    slots = (keys.shape[0] // 2) * PEER_TOPK
    tm = min(ROUTE_TM, n)
    return pl.pallas_call(
        functools.partial(_route_body, n_keys),
        grid=(n // tm,),
        in_specs=[pl.BlockSpec((tm, d), lambda i: (i, 0)),
                  pl.BlockSpec(w_q.shape, lambda i: (0, 0)),
                  pl.BlockSpec(keys.shape, lambda i: (0, 0, 0))],
        out_specs=[pl.BlockSpec((tm, slots), lambda i: (i, 0))] * 2,
        out_shape=[jax.ShapeDtypeStruct((n, slots), I32), jax.ShapeDtypeStruct((n, slots), F32)],
        compiler_params=_params("parallel"),
        name="peer_route",
    )(x1, w_q, keys)


def _gelu(x):
    return 0.5 * x * (1.0 + lax.erf(x * (2.0 ** -0.5)))


def _pack_tables(u, v):
    e, d = u.shape
    return jnp.concatenate([u.reshape(e, d // LANES, LANES), v.reshape(e, d // LANES, LANES)],
                           axis=1).reshape(-1, LANES)


def _gather_body(alpha, x_ref, gate_ref, idx_hbm, tab_hbm, g_ref, b_ref, o_ref, idx_s, buf, sem_i, sem):
    ts, slots = gate_ref.shape
    d = x_ref.shape[1]
    nw = d // LANES
    erows = 2 * nw
    tb = GATHER_TB
    rows = tb * slots
    nsub = ts // tb
    step = pl.program_id(0)
    nsteps = pl.num_programs(0)
    ib = step % 2
    nbuf = GATHER_AHEAD + 1
    kgrp = slots // (2 * nw)

    def idx_copy(s, b):
        return pltpu.make_async_copy(idx_hbm.at[pl.ds(s * ts, ts), :], idx_s.at[b], sem_i.at[b])

    def issue1(b, row, t, k, slot):
        e = idx_s[b, row, k]
        src = tab_hbm.at[pl.ds(pl.multiple_of(e * erows, erows), erows), :]
        dst = buf.at[slot, pl.ds((t * slots + k) * GATHER_PITCH, erows), :]
        pltpu.make_async_copy(src, dst, sem.at[slot]).start(priority=k % 2)

    def issue(b, row, t, k0, slot):
        for k in range(k0, k0 + kgrp):
            issue1(b, row, t, k, slot)

    def wait(slot):
        pltpu.make_async_copy(tab_hbm.at[pl.ds(0, rows * erows), :],
                              buf.at[slot, pl.ds(0, rows * erows), :], sem.at[slot]).wait()

    @pl.when(step == 0)
    def _():
        idx_copy(0, 0).start()
        idx_copy(0, 0).wait()
        for j in range(GATHER_AHEAD):
            for t in range(tb):
                for k0 in range(0, slots, kgrp):
                    issue(0, j * tb + t, t, k0, j)

    @pl.when(step + 1 < nsteps)
    def _():
        idx_copy(step + 1, 1 - ib).start()

    eye = (lax.broadcasted_iota(I32, (slots, slots), 0)
           == lax.broadcasted_iota(I32, (slots, slots), 1)).astype(F32)
    rio = lax.broadcasted_iota(I32, (tb, 1), 0)
    tio = lax.broadcasted_iota(I32, (slots, tb), 1)

    def compute(j, slot, nb, nj, nslot):
        todo = iter([(t, k) for t in range(tb) for k in range(slots)])

        def emit(n):
            for _ in range(n):
                tk = next(todo, None)
                if tk is not None:
                    issue1(nb, nj * tb + tk[0], tk[0], tk[1], nslot)

        r0 = pl.multiple_of(j * tb, tb)
        xb = x_ref[pl.ds(r0, tb), :]
        emit(GATHER_EMIT[0])
        gcol = _nt(eye, gate_ref[pl.ds(r0, tb), :], HI)
        sg = slots // GATHER_SPLIT
        chunk = lambda t, c, s: buf[slot, pl.ds((t * slots + s * sg) * GATHER_PITCH + c, sg, stride=GATHER_PITCH), :]
        hparts = [jnp.zeros((sg, tb), F32) for _ in range(GATHER_SPLIT)]
        tio_g = lax.broadcasted_iota(I32, (sg, tb), 1)
        for t in range(tb):
            for s in range(GATHER_SPLIT):
                pu = None
                for c in range(nw):
                    emit(GATHER_EMIT[1])
                    term = chunk(t, c, s) * xb[t:t + 1, c * LANES:(c + 1) * LANES]
                    pu = term if pu is None else pu + term
                hparts[s] = jnp.where(tio_g == t, jnp.sum(pu, axis=1, keepdims=True), hparts[s])
        emit(GATHER_EMIT[2])
        wmat = _gelu(jnp.concatenate(hparts, axis=0)) * gcol
        acc = jnp.zeros(xb.shape, F32)
        for t in range(tb):
            outs = [None] * nw
            for s in range(GATHER_SPLIT):
                w = wmat[s * sg:(s + 1) * sg, t:t + 1]
                for c in range(nw):
                    emit(GATHER_EMIT[3])
                    part = jnp.sum(chunk(t, nw + c, s) * w, axis=0, keepdims=True)
                    outs[c] = part if outs[c] is None else outs[c] + part
            acc = jnp.where(rio == t, jnp.concatenate(outs, axis=1), acc)
        emit(rows)
        o_ref[pl.ds(r0, tb), :] = _layer_norm(alpha * xb + acc, g_ref[...], b_ref[...])

    more = step + 1 < nsteps
    g0 = step * nsub

    def sub(j, carry):
        @pl.when(jnp.logical_and(j == nsub - GATHER_AHEAD, more))
        def _():
            idx_copy(step + 1, 1 - ib).wait()

        slot = (g0 + j) % nbuf
        wait(slot)
        jn = j + GATHER_AHEAD
        over = jn >= nsub
        compute(j, slot, jnp.where(jnp.logical_and(over, more), 1 - ib, ib), jnp.where(over, jn - nsub, jn),
                (g0 + jn) % nbuf)
        return carry

    lax.fori_loop(0, nsub, sub, 0)

    @pl.when(jnp.logical_not(more))
    def _():
        for a in range(GATHER_AHEAD):
            wait((g0 + nsub + a) % nbuf)


def _peer_gather(alpha, x1, gates, idx, tab, ln_g, ln_b):
    n, d = x1.shape
    slots = gates.shape[1]
    ts = min(GATHER_TS, n)
    rows = GATHER_TB * slots
    any_spec = pl.BlockSpec(memory_space=pl.ANY)
    return pl.pallas_call(
        functools.partial(_gather_body, alpha),
        grid=(n // ts,),
        in_specs=[pl.BlockSpec((ts, d), lambda i: (i, 0)),
                  pl.BlockSpec((ts, slots), lambda i: (i, 0)),
                  any_spec, any_spec,
                  pl.BlockSpec((1, d), lambda i: (0, 0)),
                  pl.BlockSpec((1, d), lambda i: (0, 0))],
        out_specs=pl.BlockSpec((ts, d), lambda i: (i, 0)),
        out_shape=jax.ShapeDtypeStruct((n, d), F32),
        scratch_shapes=[pltpu.SMEM((2, ts, slots), I32),
                        pltpu.VMEM((GATHER_AHEAD + 1, rows * GATHER_PITCH, LANES), F32),
                        pltpu.SemaphoreType.DMA((2,)),
                        pltpu.SemaphoreType.DMA((GATHER_AHEAD + 1,))],
        compiler_params=_params("arbitrary"),
        name="peer_gather",
    )(x1, gates, idx, tab, ln_g, ln_b)


def _pad_rows(w, n):
    return jnp.pad(w, ((0, n - w.shape[0]), (0, 0)))


def kernel(x, w_in, fox_f_bias, rwkv_mu, rwkv_w0, rwkv_w2, rwkv_a0, rwkv_a2, rwkv_g2, rwkv_k_k, rwkv_k_a, rwkv_r_k, rwkv_ln_g, rwkv_ln_b, p_fox, p_rwkv, w_o, ln1_g, ln1_b, peer_w_q, peer_sub_keys, peer_u, peer_v, ln2_g, ln2_b):
    bsz, seq, d = x.shape
    depth = w_in.shape[0]
    n = bsz * seq
    fox_heads = fox_f_bias.shape[1]
    fw = p_fox.shape[1]
    rw = p_rwkv.shape[1]
    w_lora, a_lora, g_lora = rwkv_w2.shape[1], rwkv_a2.shape[1], rwkv_g2.shape[1]
    fox_cols = 3 * fw + fox_heads
    rwkv_cols = 3 * rw + w_lora + a_lora + g_lora
    alpha = (2 * depth) ** 0.25
    hp = fw // LANES

    xcur = x.reshape(n, d)
    for l in range(depth):
        w = w_in[l]
        w_qkv = w[:, :3 * fw].astype(BF16)
        wf_t = w[:, 3 * fw:fox_cols].T.astype(BF16)
        wr = w[:, fox_cols:fox_cols + rwkv_cols]
        zpad = lambda m, c: jnp.pad(m, ((0, 0), (0, c - m.shape[1])))
        o_w = 3 * rw
        w_rw = jnp.concatenate([wr[:, :o_w],
                                zpad(wr[:, o_w:o_w + w_lora], LANES),
                                zpad(wr[:, o_w + w_lora:o_w + w_lora + a_lora], LANES),
                                zpad(wr[:, o_w + w_lora + a_lora:], LANES)], axis=1).astype(BF16)
        mu = rwkv_mu[l]
        mu_p = jnp.concatenate([mu[:o_w],
                                jnp.pad(mu[o_w:o_w + w_lora], (0, LANES - w_lora)),
                                jnp.pad(mu[o_w + w_lora:o_w + w_lora + a_lora], (0, LANES - a_lora)),
                                jnp.pad(mu[o_w + w_lora + a_lora:], (0, LANES - g_lora))])[None, :]
        w_gate = w[:, fox_cols + rwkv_cols:].astype(BF16)

        qkv, rwp, gates = _project(xcur, w_qkv, w_rw, w_gate)

        c_row = _forget_cumsum(xcur.reshape(bsz, seq, d), wf_t, fox_f_bias[l][:, None])
        c_row = c_row.reshape(bsz, hp, 2, seq)
        c_col = jnp.swapaxes(c_row, 2, 3)
        y_fox = _fox_attention(qkv.reshape(bsz, seq, 3 * fw), c_col, c_row, bsz, seq, fw)

        row = lambda a: a[l][None, :]
        y_rwkv = _rwkv_scan(rwp.reshape(bsz, seq, -1), mu_p, row(rwkv_w0), row(rwkv_a0), row(rwkv_k_k),
                            row(rwkv_k_a), row(rwkv_r_k), _pad_rows(rwkv_w2[l], LANES),
                            _pad_rows(rwkv_a2[l], LANES), rwkv_g2[l], row(rwkv_ln_g), row(rwkv_ln_b))

        x1 = _merge(alpha, xcur, y_fox.reshape(n, fw), y_rwkv.reshape(n, rw), gates,
                    p_fox[l].astype(BF16), p_rwkv[l].astype(BF16), w_o[l].astype(BF16),
                    row(ln1_g), row(ln1_b))

        keys = peer_sub_keys[l]
        n_keys, half = keys.shape[2], keys.shape[3]
        idx, pg = _peer_route(x1, peer_w_q[l].astype(BF16),
                              keys.reshape(-1, n_keys, half).astype(BF16))
        xcur = _peer_gather(alpha, x1, pg, idx, _pack_tables(peer_u[l], peer_v[l]), row(ln2_g), row(ln2_b))
    return xcur.reshape(bsz, seq, d)
```

```python
import functools
import math

import jax
import jax.numpy as jnp
from jax import lax
from jax.experimental import pallas as pl
from jax.experimental.pallas import tpu as pltpu

F32 = jnp.float32
BF16 = jnp.bfloat16
I32 = jnp.int32
HI = lax.Precision.HIGHEST

LANES = 128
HEAD_DIM = 64
PEER_TOPK = 16
LN_EPS = 1e-5
GN_EPS = 64e-5
NEG_BIG = -1e30
VMEM_LIMIT = 56 * 1024 * 1024

MM_TM = 256
FG_L = 512
ATT_BQ = 1024
ATT_RB = 256
SCAN_SB = 1024
SCAN_C = 64
MERGE_TM = 512
ROUTE_TM = 256
GATHER_TS = 128
GATHER_TB = 8
GATHER_AHEAD = 2
GATHER_SPLIT = 2
GATHER_EMIT = (0, 4, 0, 4)
GATHER_PITCH = 20


def _nt(a, b, precision=None):
    return lax.dot_general(a, b, (((1,), (1,)), ((), ())), precision=precision,
                           preferred_element_type=F32)


_NN = ((1,), (0,))
_NT = ((1,), (1,))
_TN = ((0,), (0,))


def _bdot(a, b, dims):
    return lax.dot_general(a.astype(BF16), b.astype(BF16), (dims, ((), ())), preferred_element_type=F32)


def _sigmoid(x):
    return 1.0 / (1.0 + jnp.exp(-x))


def _params(*sem):
    return pltpu.CompilerParams(dimension_semantics=sem, vmem_limit_bytes=VMEM_LIMIT)


def _proj_body(x_ref, wq_ref, wr_ref, wg_ref, q_o, r_o, g_o):
    xb = x_ref[...].astype(BF16)
    q_o[...] = jnp.dot(xb, wq_ref[...], preferred_element_type=F32).astype(q_o.dtype)
    r_o[...] = jnp.dot(xb, wr_ref[...], preferred_element_type=F32)
    g_o[...] = jnp.dot(xb, wg_ref[...], preferred_element_type=F32)


def _project(x, w_qkv, w_rw, w_gate):
    m, k = x.shape
    tm = min(MM_TM, m)
    full = lambda w: pl.BlockSpec(w.shape, lambda i: (0, 0))
    out = lambda w: pl.BlockSpec((tm, w.shape[1]), lambda i: (i, 0))
    return pl.pallas_call(
        _proj_body,
        grid=(m // tm,),
        in_specs=[pl.BlockSpec((tm, k), lambda i: (i, 0)), full(w_qkv), full(w_rw), full(w_gate)],
        out_specs=[out(w_qkv), out(w_rw), out(w_gate)],
        out_shape=[jax.ShapeDtypeStruct((m, w_qkv.shape[1]), BF16),
                   jax.ShapeDtypeStruct((m, w_rw.shape[1]), F32),
                   jax.ShapeDtypeStruct((m, w_gate.shape[1]), F32)],
        compiler_params=_params("parallel"),
        name="in_proj",
    )(x, w_qkv, w_rw, w_gate)


def _fgate_body(x_ref, wf_ref, b_ref, c_ref, carry_ref):
    @pl.when(pl.program_id(1) == 0)
    def _():
        carry_ref[...] = jnp.zeros_like(carry_ref)

    xb = x_ref[0].astype(BF16)
    f = _nt(wf_ref[...], xb) + b_ref[...]
    logf = -(jnp.maximum(-f, 0.0) + jnp.log1p(jnp.exp(-jnp.abs(f))))
    n = f.shape[1]
    tri = (lax.broadcasted_iota(I32, (n, n), 0) <= lax.broadcasted_iota(I32, (n, n), 1)).astype(F32)
    cs = jnp.dot(logf, tri, precision=HI, preferred_element_type=F32) + carry_ref[:, 0:1]
    c_ref[0] = cs
    carry_ref[...] = jnp.broadcast_to(cs[:, n - 1:n], carry_ref.shape)


def _forget_cumsum(x3, wf_t, bias):
    b, s, d = x3.shape
    h = wf_t.shape[0]
    blk = min(FG_L, s)
    return pl.pallas_call(
        _fgate_body,
        grid=(b, s // blk),
        in_specs=[pl.BlockSpec((1, blk, d), lambda i, j: (i, j, 0)),
                  pl.BlockSpec((h, d), lambda i, j: (0, 0)),
                  pl.BlockSpec((h, 1), lambda i, j: (0, 0))],
        out_specs=pl.BlockSpec((1, h, blk), lambda i, j: (i, 0, j)),
        out_shape=jax.ShapeDtypeStruct((b, h, s), F32),
        scratch_shapes=[pltpu.VMEM((h, LANES), F32)],
        compiler_params=_params("parallel", "arbitrary"),
        name="forget_cumsum",
    )(x3, wf_t, bias)


def _fox_body(qi_tab, ki_tab, q_ref, k_ref, v_ref, cq_ref, ck_ref, o_ref,
              qh_s, m_s, l_s, acc_s):
    p = pl.program_id(2)
    qi = qi_tab[p]
    ki = ki_tab[p]
    bq = q_ref.shape[1]
    bk = k_ref.shape[1]
    lane = lax.broadcasted_iota(I32, (1, LANES), 1)

    @pl.when(ki == 0)
    def _():
        q = q_ref[0] * jnp.asarray(HEAD_DIM ** -0.5, BF16)
        zero = jnp.zeros_like(q)
        qh_s[0] = jnp.where(lane < HEAD_DIM, q, zero)
        qh_s[1] = jnp.where(lane >= HEAD_DIM, q, zero)
        m_s[...] = jnp.full_like(m_s, NEG_BIG)
        l_s[...] = jnp.zeros_like(l_s)
        acc_s[...] = jnp.zeros_like(acc_s)

    def step(masked):
        rb = min(ATT_RB, bq)
        units = [(r, h) for h in range(2) for r in range(bq // rb)]

        def scores(unit):
            r, h = unit
            z = _nt(qh_s[h, r * rb:(r + 1) * rb, :], k_ref[0]) - ck_ref[0, 0][h:h + 1, :]
            if masked:
                rows = lax.broadcasted_iota(I32, (rb, bk), 0) + r * rb
                cols = lax.broadcasted_iota(I32, (rb, bk), 1)
                z = jnp.where(rows >= cols, z, NEG_BIG)
            return z

        def finish(unit, z):
            r, h = unit
            rs = slice(r * rb, (r + 1) * rb)
            cq = cq_ref[0, 0, rs, :][:, h:h + 1]
            m_prev = m_s[h, rs, :]
            m_new = jnp.maximum(m_prev, jnp.max(z, axis=1, keepdims=True) + cq)
            alpha = jnp.exp(m_prev - m_new)
            pr = jnp.exp(z - (m_new - cq))
            l_s[h, rs, :] = alpha * l_s[h, rs, :] + jnp.sum(pr, axis=1, keepdims=True)
            acc_s[h, rs, :] = alpha * acc_s[h, rs, :] + jnp.dot(
                pr.astype(BF16), v_ref[0], preferred_element_type=F32)
            m_s[h, rs, :] = m_new

        z = scores(units[0])
        for i, unit in enumerate(units):
            z_next = scores(units[i + 1]) if i + 1 < len(units) else None
            finish(unit, z)
            z = z_next

    @pl.when(ki < qi)
    def _():
        step(False)

    @pl.when(ki == qi)
    def _():
        step(True)
        o0 = acc_s[0] / l_s[0]
        o1 = acc_s[1] / l_s[1]
        o_ref[0] = jnp.where(lane < HEAD_DIM, o0, o1).astype(o_ref.dtype)


def _fox_attention(qkv, c_col, c_row, b, s, width):
    hp = width // LANES
    bq = min(ATT_BQ, s)
    nq = s // bq
    pairs = [(i, j) for i in range(nq) for j in range(i + 1)]
    qi_tab = jnp.asarray([pq for pq, _ in pairs], I32)
    ki_tab = jnp.asarray([pk for _, pk in pairs], I32)
    grid_spec = pltpu.PrefetchScalarGridSpec(
        num_scalar_prefetch=2,
        grid=(b, hp, len(pairs)),
        in_specs=[
            pl.BlockSpec((1, bq, LANES), lambda i, j, p, qt, kt: (i, qt[p], j)),
            pl.BlockSpec((1, bq, LANES), lambda i, j, p, qt, kt: (i, kt[p], hp + j)),
            pl.BlockSpec((1, bq, LANES), lambda i, j, p, qt, kt: (i, kt[p], 2 * hp + j)),
            pl.BlockSpec((1, 1, bq, 2), lambda i, j, p, qt, kt: (i, j, qt[p], 0)),
            pl.BlockSpec((1, 1, 2, bq), lambda i, j, p, qt, kt: (i, j, 0, kt[p])),
        ],
        out_specs=pl.BlockSpec((1, bq, LANES), lambda i, j, p, qt, kt: (i, qt[p], j)),
        scratch_shapes=[pltpu.VMEM((2, bq, LANES), BF16),
                        pltpu.VMEM((2, bq, 1), F32),
                        pltpu.VMEM((2, bq, 1), F32),
                        pltpu.VMEM((2, bq, LANES), F32)],
    )
    return pl.pallas_call(
        _fox_body,
        grid_spec=grid_spec,
        out_shape=jax.ShapeDtypeStruct((b, s, width), BF16),
        compiler_params=_params("parallel", "parallel", "arbitrary"),
        name="fox_attention",
    )(qi_tab, ki_tab, qkv, qkv, qkv, c_col, c_row)


def _rwkv_features(shifted, w0, a0, k_k, k_a, r_k, w2, a2, g2, seg):
    r, k, v, wl, al, gl = shifted
    hdot = lambda x, y: _bdot(x, y, _NN)
    w_pre = w0 + hdot(jnp.tanh(wl), w2)
    lw = (-math.exp(-0.5)) * _sigmoid(w_pre)
    a = _sigmoid(a0 + hdot(al, a2))
    g = hdot(_sigmoid(gl), g2)
    kk = k * k_k
    kkn = kk / jnp.maximum(jnp.sqrt(hdot(kk * kk, seg)), 1e-12)
    kmod = k * (1.0 + (a - 1.0) * k_a)
    bonus = hdot(r * kmod * r_k, seg) * v
    return r, lw, kmod, v, kkn, kkn * a, bonus, g


def _scan_body(*refs):
    p_refs, mu_refs = refs[0:6], refs[6:12]
    w0_ref, a0_ref, kk_ref, ka_ref, rk_ref, w2_ref, a2_ref, g2_ref, lng_ref, lnb_ref = refs[12:22]
    y_ref, s_ref, carry_ref = refs[22:25]
    sb = y_ref.shape[1]
    c = min(SCAN_C, sb)

    @pl.when(pl.program_id(2) == 0)
    def _():
        s_ref[...] = jnp.zeros_like(s_ref)
        carry_ref[...] = jnp.zeros_like(carry_ref)

    lane = lax.broadcasted_iota(I32, (1, LANES), 1)
    head0 = lane < HEAD_DIM
    ti = lax.broadcasted_iota(I32, (c, c), 0)
    si = lax.broadcasted_iota(I32, (c, c), 1)
    incl = ti >= si
    strict = ti > si
    ltri = incl.astype(BF16)
    eye = (ti == si).astype(F32)
    ri = lax.broadcasted_iota(I32, (LANES, LANES), 0)
    ci = lax.broadcasted_iota(I32, (LANES, LANES), 1)
    same_head = (ri < HEAD_DIM) == (ci < HEAD_DIM)
    seg_mean = same_head.astype(F32) * (1.0 / HEAD_DIM)
    nchunk = sb // c
    sel = lambda x0, x1: jnp.where(head0, x0, x1)


    row0 = lax.broadcasted_iota(I32, (sb, LANES), 0) == 0
    shifted = []
    for i in range(6):
        p = p_refs[i][0]
        prev = jnp.where(row0, jnp.broadcast_to(carry_ref[i, 0:1, :], p.shape), pltpu.roll(p, 1, 0))
        carry_ref[i] = jnp.broadcast_to(p[sb - 1:sb, :], carry_ref.shape[1:])
        shifted.append(p + mu_refs[i][...] * (prev - p))
    r_all, lw_all, k_all, v_all, kk_all, b_all, bonus, gate = _rwkv_features(
        shifted, w0_ref[...], a0_ref[...], kk_ref[...], ka_ref[...], rk_ref[...],
        w2_ref[...], a2_ref[...], g2_ref[...], same_head.astype(F32))

    am, bm, km, rm, vv, glast = [], [], [], [], [], []
    for ch in range(nchunk):
        sl = slice(ch * c, (ch + 1) * c)
        lw = lw_all[sl]
        l1 = lw.astype(BF16)
        r1 = lw - l1.astype(F32)
        l2 = r1.astype(BF16)
        l3 = (r1 - l2.astype(F32)).astype(BF16)
        tri_dot = lambda x: jnp.dot(ltri, x, preferred_element_type=F32)
        cum = tri_dot(l1) + (tri_dot(l2) + tri_dot(l3))
        g_in = jnp.exp(cum)
        g_inv = jnp.exp(-cum)
        am.append(kk_all[sl] * jnp.exp(cum - lw))
        bm.append(b_all[sl] * g_inv)
        km.append(k_all[sl] * g_inv)
        rm.append(r_all[sl] * g_in)
        vv.append(v_all[sl])
        glast.append(g_in[c - 1:c, :])

    lak, mrb, mrk, pw, tt = [], [], [], [], []
    for ch in range(nchunk):
        for h in range(2):
            msk = head0 if h == 0 else jnp.logical_not(head0)
            a_h = jnp.where(msk, am[ch], 0.0)
            r_h = jnp.where(msk, rm[ch], 0.0)
            lab = jnp.where(strict, _bdot(a_h, bm[ch], _NT), 0.0)
            lak.append(jnp.where(strict, _bdot(a_h, km[ch], _NT), 0.0))
            mrb.append(jnp.where(incl, _bdot(r_h, bm[ch], _NT), 0.0))
            mrk.append(jnp.where(incl, _bdot(r_h, km[ch], _NT), 0.0))
            pw.append(-lab)
            tt.append(eye - lab)
    for _ in range(max(1, int(math.ceil(math.log2(c))) - 1)):
        pw = [_bdot(p_, p_, _NN) for p_ in pw]
        tt = [t_ + _bdot(t_, p_, _NN) for t_, p_ in zip(tt, pw)]

    rp, y0, pmat, qmat = [], [], [], []
    for ch in range(nchunk):
        t0, t1 = tt[2 * ch], tt[2 * ch + 1]
        ta = sel(_bdot(t0, am[ch], _NN), _bdot(t1, am[ch], _NN))
        w = sel(_bdot(t0, _bdot(lak[2 * ch], vv[ch], _NN), _NN),
                _bdot(t1, _bdot(lak[2 * ch + 1], vv[ch], _NN), _NN))
        rp.append(rm[ch] - sel(_bdot(mrb[2 * ch], ta, _NN), _bdot(mrb[2 * ch + 1], ta, _NN)))
        y0.append(sel(_bdot(mrk[2 * ch], vv[ch], _NN) - _bdot(mrb[2 * ch], w, _NN),
                      _bdot(mrk[2 * ch + 1], vv[ch], _NN) - _bdot(mrb[2 * ch + 1], w, _NN)))
        pmat.append(jnp.where(same_head, -_bdot(ta, bm[ch], _TN), 0.0))
        qmat.append(jnp.where(same_head, _bdot(vv[ch], km[ch], _TN) - _bdot(w, bm[ch], _TN), 0.0))

    st = s_ref[...]
    ys = []
    for ch in range(nchunk):
        ys.append(_bdot(rp[ch], st, _NT) + y0[ch])
        st = (st + _bdot(st, pmat[ch], _NN) + qmat[ch]) * glast[ch]
    s_ref[...] = st

    y = jnp.concatenate(ys, axis=0)
    mu = _bdot(y, seg_mean, _NN)
    yc = y - mu
    var = _bdot(yc * yc, seg_mean, _NN)
    yn = yc * lax.rsqrt(var + GN_EPS) * lng_ref[...] + lnb_ref[...]
    y_ref[0] = ((yn + bonus) * gate).astype(y_ref.dtype)


def _rwkv_scan(proj, mu, w0, a0, k_k, k_a, r_k, w2p, a2p, g2, ln_g, ln_b):
    b, s, _ = proj.shape
    width = w0.shape[1]
    hp = width // LANES
    sb = min(SCAN_SB, s)
    cols = [lambda j: j, lambda j: hp + j, lambda j: 2 * hp + j,
            lambda j: 3 * hp, lambda j: 3 * hp + 1, lambda j: 3 * hp + 2]
    p_specs = [pl.BlockSpec((1, sb, LANES), lambda i, j, t, f=f: (i, t, f(j))) for f in cols]
    mu_specs = [pl.BlockSpec((1, LANES), lambda i, j, t, f=f: (0, f(j))) for f in cols]
    vec = pl.BlockSpec((1, LANES), lambda i, j, t: (0, j))
    mat = pl.BlockSpec((LANES, LANES), lambda i, j, t: (0, j))
    return pl.pallas_call(
        _scan_body,
        grid=(b, hp, s // sb),
        in_specs=p_specs + mu_specs + [vec] * 5 + [mat] * 3 + [vec, vec],
        out_specs=pl.BlockSpec((1, sb, LANES), lambda i, j, t: (i, t, j)),
        out_shape=jax.ShapeDtypeStruct((b, s, width), BF16),
        scratch_shapes=[pltpu.VMEM((LANES, LANES), F32), pltpu.VMEM((6, 8, LANES), F32)],
        compiler_params=_params("parallel", "parallel", "arbitrary"),
        name="rwkv_scan",
    )(*([proj] * 6), *([mu] * 6), w0, a0, k_k, k_a, r_k, w2p, a2p, g2, ln_g, ln_b)


def _layer_norm(y, g, b):
    mu = jnp.mean(y, axis=-1, keepdims=True)
    yc = y - mu
    var = jnp.mean(yc * yc, axis=-1, keepdims=True)
    return yc * lax.rsqrt(var + LN_EPS) * g + b


def _merge_body(alpha, x_ref, yf_ref, yr_ref, gf_ref, gr_ref, pf_ref, pr_ref, wo_ref, g_ref, b_ref, o_ref):
    mf = jnp.dot(yf_ref[...], pf_ref[...], preferred_element_type=F32)
    mr = jnp.dot(yr_ref[...], pr_ref[...], preferred_element_type=F32)
    merged = _sigmoid(gf_ref[...]) * mf + _sigmoid(gr_ref[...]) * mr
    out = jnp.dot(merged.astype(BF16), wo_ref[...], preferred_element_type=F32)
    o_ref[...] = _layer_norm(alpha * x_ref[...] + out, g_ref[...], b_ref[...])


def _merge(alpha, x2, y_fox, y_rwkv, gates, p_fox, p_rwkv, w_o, ln_g, ln_b):
    n, d = x2.shape
    wf = y_fox.shape[1]
    wr = y_rwkv.shape[1]
    tm = min(MERGE_TM, n)
    row = lambda w, c=0: pl.BlockSpec((tm, w), lambda i: (i, c))
    full = lambda a: pl.BlockSpec(a.shape, lambda i: (0, 0))
    return pl.pallas_call(
        functools.partial(_merge_body, alpha),
        grid=(n // tm,),
        in_specs=[row(d), row(wf), row(wr), row(d, 0), row(d, 1),
                  full(p_fox), full(p_rwkv), full(w_o), full(ln_g), full(ln_b)],
        out_specs=row(d),
        out_shape=jax.ShapeDtypeStruct((n, d), F32),
        compiler_params=_params("parallel"),
        name="merge_ln",
    )(x2, y_fox, y_rwkv, gates, gates, p_fox, p_rwkv, w_o, ln_g, ln_b)


def _topk_rows(vals, k, payload=None):
    n, t = vals.shape
    rio = lax.broadcasted_iota(I32, (n, t), 0).astype(F32)
    kio = lax.broadcasted_iota(I32, (k, t), 0)
    out_v = jnp.zeros((k, t), F32)
    out_i = jnp.zeros((k, t), I32)
    cur = vals
    for it in range(k):
        mx = jnp.max(cur, axis=0, keepdims=True)
        sel = jnp.min(jnp.where(cur == mx, rio, float(n)), axis=0, keepdims=True)
        hit = rio == sel
        if payload is None:
            got = sel.astype(I32)
        else:
            got = jnp.sum(jnp.where(hit, payload, 0), axis=0, keepdims=True)
        out_v = jnp.where(kio == it, mx, out_v)
        out_i = jnp.where(kio == it, got, out_i)
        cur = jnp.where(hit, -jnp.inf, cur)
    return out_v, out_i


def _pair_candidates(top0, top1, combine):
    assert top0.shape[0] == 16 and top1.shape[0] == 16
    sub = lax.broadcasted_iota(I32, (8, top0.shape[1]), 0)
    lo = top1[0:8]
    lo_up = pltpu.roll(lo, 4, 0)
    pieces = [combine(top0[0:1], top1)]
    pieces += [combine(top0[a:a + 1], lo) for a in (1, 2, 3)]
    pieces += [jnp.where(sub < 4, combine(top0[a:a + 1], lo), combine(top0[a + 1:a + 2], lo_up))
               for a in (4, 6)]
    pieces.append(combine(top0[8:16], top1[0:1]))
    return jnp.concatenate(pieces, axis=0)


def _route_body(n_keys, x_ref, wq_ref, keys_ref, idx_ref, gate_ref):
    heads = keys_ref.shape[0] // 2
    half = keys_ref.shape[2]
    kt = PEER_TOPK
    q = jnp.dot(x_ref[...].astype(BF16), wq_ref[...], preferred_element_type=F32)
    idx_rows, gate_rows = [], []
    for h in range(heads):
        tops = []
        for p in range(2):
            c = 2 * h + p
            qc = q[:, c * half:(c + 1) * half].astype(BF16)
            s_t = _nt(keys_ref[c], qc)
            tops.append(_topk_rows(s_t, kt))
        (s0, i0), (s1, i1) = tops
        cand_s = _pair_candidates(s0, s1, lambda x, y: x + y)
        cand_i = _pair_candidates(i0, i1, lambda x, y: x * n_keys + y)
        best_s, best_i = _topk_rows(cand_s, kt, cand_i)
        e = jnp.exp(best_s - best_s[0:1, :])
        gate_rows.append(e / jnp.sum(e, axis=0, keepdims=True))
        idx_rows.append(best_i)
    idx_ref[...] = jnp.concatenate(idx_rows, axis=0).T
    gate_ref[...] = jnp.concatenate(gate_rows, axis=0).T


def _peer_route(x1, w_q, keys):
    n, d = x1.shape
    n_keys = keys.shape[1]
    slots = (keys.shape[0] // 2) * PEER_TOPK
    tm = min(ROUTE_TM, n)
    return pl.pallas_call(
        functools.partial(_route_body, n_keys),
        grid=(n // tm,),
        in_specs=[pl.BlockSpec((tm, d), lambda i: (i, 0)),
                  pl.BlockSpec(w_q.shape, lambda i: (0, 0)),
                  pl.BlockSpec(keys.shape, lambda i: (0, 0, 0))],
        out_specs=[pl.BlockSpec((tm, slots), lambda i: (i, 0))] * 2,
        out_shape=[jax.ShapeDtypeStruct((n, slots), I32), jax.ShapeDtypeStruct((n, slots), F32)],
        compiler_params=_params("parallel"),
        name="peer_route",
    )(x1, w_q, keys)


def _gelu(x):
    return 0.5 * x * (1.0 + lax.erf(x * (2.0 ** -0.5)))


def _pack_tables(u, v):
    e, d = u.shape
    return jnp.concatenate([u.reshape(e, d // LANES, LANES), v.reshape(e, d // LANES, LANES)],
                           axis=1).reshape(-1, LANES)


def _gather_body(alpha, x_ref, gate_ref, idx_hbm, tab_hbm, g_ref, b_ref, o_ref, idx_s, buf, sem_i, sem):
    ts, slots = gate_ref.shape
    d = x_ref.shape[1]
    nw = d // LANES
    erows = 2 * nw
    tb = GATHER_TB
    rows = tb * slots
    nsub = ts // tb
    step = pl.program_id(0)
    nsteps = pl.num_programs(0)
    ib = step % 2
    nbuf = GATHER_AHEAD + 1
    kgrp = slots // (2 * nw)

    def idx_copy(s, b):
        return pltpu.make_async_copy(idx_hbm.at[pl.ds(s * ts, ts), :], idx_s.at[b], sem_i.at[b])

    def issue1(b, row, t, k, slot):
        e = idx_s.at[b, row][k]
        src = tab_hbm.at[pl.ds(pl.multiple_of(e * erows, erows), erows), :]
        dst = buf.at[slot, pl.ds((t * slots + k) * GATHER_PITCH, erows), :]
        pltpu.make_async_copy(src, dst, sem.at[slot]).start(priority=k % 2)

    def issue(b, row, t, k0, slot):
        for k in range(k0, k0 + kgrp):
            issue1(b, row, t, k, slot)

    def wait(slot):
        pltpu.make_async_copy(tab_hbm.at[pl.ds(0, rows * erows), :],
                              buf.at[slot, pl.ds(0, rows * erows), :], sem.at[slot]).wait()

    @pl.when(step == 0)
    def _():
        idx_copy(0, 0).start()
        idx_copy(0, 0).wait()
        for j in range(GATHER_AHEAD):
            for t in range(tb):
                for k0 in range(0, slots, kgrp):
                    issue(0, j * tb + t, t, k0, j)

    @pl.when(step + 1 < nsteps)
    def _():
        idx_copy(step + 1, 1 - ib).start()

    eye = (lax.broadcasted_iota(I32, (slots, slots), 0)
           == lax.broadcasted_iota(I32, (slots, slots), 1)).astype(F32)
    rio = lax.broadcasted_iota(I32, (tb, 1), 0)
    tio = lax.broadcasted_iota(I32, (slots, tb), 1)

    def compute(j, slot, nb, nj, nslot):
        todo = iter([(t, k) for t in range(tb) for k in range(slots)])

        def emit(n):
            for _ in range(n):
                tk = next(todo, None)
                if tk is not None:
                    issue1(nb, nj * tb + tk[0], tk[0], tk[1], nslot)

        r0 = pl.multiple_of(j * tb, tb)
        xb = x_ref[pl.ds(r0, tb), :]
        emit(GATHER_EMIT[0])
        gcol = _nt(eye, gate_ref[pl.ds(r0, tb), :], HI)
        sg = slots // GATHER_SPLIT
        chunk = lambda t, c, s: buf[slot, pl.ds((t * slots + s * sg) * GATHER_PITCH + c, sg, stride=GATHER_PITCH), :]
        hparts = [jnp.zeros((sg, tb), F32) for _ in range(GATHER_SPLIT)]
        tio_g = lax.broadcasted_iota(I32, (sg, tb), 1)
        for t in range(tb):
            for s in range(GATHER_SPLIT):
                pu = None
                for c in range(nw):
                    emit(GATHER_EMIT[1])
                    term = chunk(t, c, s) * xb[t:t + 1, c * LANES:(c + 1) * LANES]
                    pu = term if pu is None else pu + term
                hparts[s] = jnp.where(tio_g == t, jnp.sum(pu, axis=1, keepdims=True), hparts[s])
        emit(GATHER_EMIT[2])
        wmat = _gelu(jnp.concatenate(hparts, axis=0)) * gcol
        acc = jnp.zeros(xb.shape, F32)
        for t in range(tb):
            outs = [None] * nw
            for s in range(GATHER_SPLIT):
                w = wmat[s * sg:(s + 1) * sg, t:t + 1]
                for c in range(nw):
                    emit(GATHER_EMIT[3])
                    part = jnp.sum(chunk(t, nw + c, s) * w, axis=0, keepdims=True)
                    outs[c] = part if outs[c] is None else outs[c] + part
            acc = jnp.where(rio == t, jnp.concatenate(outs, axis=1), acc)
        emit(rows)
        o_ref[pl.ds(r0, tb), :] = _layer_norm(alpha * xb + acc, g_ref[...], b_ref[...])

    more = step + 1 < nsteps
    g0 = step * nsub

    def sub(j, carry):
        @pl.when(jnp.logical_and(j == nsub - GATHER_AHEAD, more))
        def _():
            idx_copy(step + 1, 1 - ib).wait()

        slot = (g0 + j) % nbuf
        wait(slot)
        jn = j + GATHER_AHEAD
        over = jn >= nsub
        compute(j, slot, jnp.where(jnp.logical_and(over, more), 1 - ib, ib), jnp.where(over, jn - nsub, jn),
                (g0 + jn) % nbuf)
        return carry

    lax.fori_loop(0, nsub, sub, 0)

    @pl.when(jnp.logical_not(more))
    def _():
        for a in range(GATHER_AHEAD):
            wait((g0 + nsub + a) % nbuf)


def _peer_gather(alpha, x1, gates, idx, tab, ln_g, ln_b):
    n, d = x1.shape
    slots = gates.shape[1]
    ts = min(GATHER_TS, n)
    rows = GATHER_TB * slots
    any_spec = pl.BlockSpec(memory_space=pl.ANY)
    return pl.pallas_call(
        functools.partial(_gather_body, alpha),
        grid=(n // ts,),
        in_specs=[pl.BlockSpec((ts, d), lambda i: (i, 0)),
                  pl.BlockSpec((ts, slots), lambda i: (i, 0)),
                  any_spec, any_spec,
                  pl.BlockSpec((1, d), lambda i: (0, 0)),
                  pl.BlockSpec((1, d), lambda i: (0, 0))],
        out_specs=pl.BlockSpec((ts, d), lambda i: (i, 0)),
        out_shape=jax.ShapeDtypeStruct((n, d), F32),
        scratch_shapes=[pltpu.SMEM((2, ts, slots), I32),
                        pltpu.VMEM((GATHER_AHEAD + 1, rows * GATHER_PITCH, LANES), F32),
                        pltpu.SemaphoreType.DMA((2,)),
                        pltpu.SemaphoreType.DMA((GATHER_AHEAD + 1,))],
        compiler_params=_params("arbitrary"),
        name="peer_gather",
    )(x1, gates, idx, tab, ln_g, ln_b)


def _pad_rows(w, n):
    return jnp.pad(w, ((0, n - w.shape[0]), (0, 0)))


def kernel(x, w_in, fox_f_bias, rwkv_mu, rwkv_w0, rwkv_w2, rwkv_a0, rwkv_a2, rwkv_g2, rwkv_k_k, rwkv_k_a, rwkv_r_k, rwkv_ln_g, rwkv_ln_b, p_fox, p_rwkv, w_o, ln1_g, ln1_b, peer_w_q, peer_sub_keys, peer_u, peer_v, ln2_g, ln2_b):
    bsz, seq, d = x.shape
    depth = w_in.shape[0]
    n = bsz * seq
    fox_heads = fox_f_bias.shape[1]
    fw = p_fox.shape[1]
    rw = p_rwkv.shape[1]
    w_lora, a_lora, g_lora = rwkv_w2.shape[1], rwkv_a2.shape[1], rwkv_g2.shape[1]
    fox_cols = 3 * fw + fox_heads
    rwkv_cols = 3 * rw + w_lora + a_lora + g_lora
    alpha = (2 * depth) ** 0.25
    hp = fw // LANES

    xcur = x.reshape(n, d)
    for l in range(depth):
        w = w_in[l]
        w_qkv = w[:, :3 * fw].astype(BF16)
        wf_t = w[:, 3 * fw:fox_cols].T.astype(BF16)
        wr = w[:, fox_cols:fox_cols + rwkv_cols]
        zpad = lambda m, c: jnp.pad(m, ((0, 0), (0, c - m.shape[1])))
        o_w = 3 * rw
        w_rw = jnp.concatenate([wr[:, :o_w],
                                zpad(wr[:, o_w:o_w + w_lora], LANES),
                                zpad(wr[:, o_w + w_lora:o_w + w_lora + a_lora], LANES),
                                zpad(wr[:, o_w + w_lora + a_lora:], LANES)], axis=1).astype(BF16)
        mu = rwkv_mu[l]
        mu_p = jnp.concatenate([mu[:o_w],
                                jnp.pad(mu[o_w:o_w + w_lora], (0, LANES - w_lora)),
                                jnp.pad(mu[o_w + w_lora:o_w + w_lora + a_lora], (0, LANES - a_lora)),
                                jnp.pad(mu[o_w + w_lora + a_lora:], (0, LANES - g_lora))])[None, :]
        w_gate = w[:, fox_cols + rwkv_cols:].astype(BF16)

        qkv, rwp, gates = _project(xcur, w_qkv, w_rw, w_gate)

        c_row = _forget_cumsum(xcur.reshape(bsz, seq, d), wf_t, fox_f_bias[l][:, None])
        c_row = c_row.reshape(bsz, hp, 2, seq)
        c_col = jnp.swapaxes(c_row, 2, 3)
        y_fox = _fox_attention(qkv.reshape(bsz, seq, 3 * fw), c_col, c_row, bsz, seq, fw)

        row = lambda a: a[l][None, :]
        y_rwkv = _rwkv_scan(rwp.reshape(bsz, seq, -1), mu_p, row(rwkv_w0), row(rwkv_a0), row(rwkv_k_k),
                            row(rwkv_k_a), row(rwkv_r_k), _pad_rows(rwkv_w2[l], LANES),
                            _pad_rows(rwkv_a2[l], LANES), rwkv_g2[l], row(rwkv_ln_g), row(rwkv_ln_b))

        x1 = _merge(alpha, xcur, y_fox.reshape(n, fw), y_rwkv.reshape(n, rw), gates,
                    p_fox[l].astype(BF16), p_rwkv[l].astype(BF16), w_o[l].astype(BF16),
                    row(ln1_g), row(ln1_b))

        keys = peer_sub_keys[l]
        n_keys, half = keys.shape[2], keys.shape[3]
        idx, pg = _peer_route(x1, peer_w_q[l].astype(BF16),
                              keys.reshape(-1, n_keys, half).astype(BF16))
        xcur = _peer_gather(alpha, x1, pg, idx, _pack_tables(peer_u[l], peer_v[l]), row(ln2_g), row(ln2_b))
    return xcur.reshape(bsz, seq, d)
```
